```python
import math
import jax, jax.numpy as jnp
from jax import lax
import numpy as np

D_MODEL = 1024
BATCH = 2
SEQ = 8192
DEPTH = 2

RMS_EPS = 1e-6
LN_EPS = 1e-5
MIX_WIDTH = D_MODEL
POOL_WIDTH = MIX_WIDTH // 2
POOL_WINDOWS = (2, 4, 8, 16)
POOL_GROUPS = len(POOL_WINDOWS)
POOL_GROUP_DIM = POOL_WIDTH // POOL_GROUPS
SSM_WIDTH = MIX_WIDTH - POOL_WIDTH
SSM_GROUP_DIM = 16
SSM_GROUPS = SSM_WIDTH // SSM_GROUP_DIM
SSM_STATE = 64
DT_MIN = 0.001
DT_MAX = 0.1
CONV_CHANNELS = D_MODEL
CONV_KERNEL = 31
N_EVEN = (DEPTH + 1) // 2
N_ODD = DEPTH // 2

kernel_name = "hybrid_pool_s5_conformer_gated"


def _rmsnorm(x, g):
    x32 = x.astype(jnp.float32)
    y = x32 * lax.rsqrt(jnp.mean(x32 * x32, axis=-1, keepdims=True) + RMS_EPS)
    return (y * g.astype(jnp.float32)).astype(x.dtype)


def _multiscale_pool(u, pool_w, pool_scale):
    b, s, _ = u.shape
    u32 = u.astype(jnp.float32).reshape(b, s, POOL_GROUPS, POOL_GROUP_DIM)
    csum = jnp.cumsum(u32, axis=1)
    pos = jnp.arange(1, s + 1, dtype=jnp.float32)[None, :, None]
    outs = []
    for g, w in enumerate(POOL_WINDOWS):
        c = csum[:, :, g, :]
        lag = jnp.pad(c, ((0, 0), (w, 0), (0, 0)))[:, :s, :]
        outs.append((c - lag) / jnp.minimum(pos, float(w)) - u32[:, :, g, :])
    pooled = jnp.stack(outs, axis=2)
    mixed = jnp.einsum('bsgc,gcd->bsgd', pooled, pool_w.astype(jnp.float32))
    return (mixed.reshape(b, s, POOL_WIDTH) * pool_scale.astype(jnp.float32)).astype(u.dtype)


def _scan_combine(left, right):
    a1r, a1i, b1r, b1i = left
    a2r, a2i, b2r, b2i = right
    ar = a2r * a1r - a2i * a1i
    ai = a2r * a1i + a2i * a1r
    br = a2r * b1r - a2i * b1i + b2r
    bi = a2r * b1i + a2i * b1r + b2i
    return (ar, ai, br, bi)


def _s5(u, log_dt, a_re, a_im, b_re, b_im, c_re, c_im, d_skip, w_glu):
    bsz, s, _ = u.shape
    f32 = jnp.float32
    u32 = u.astype(f32).reshape(bsz, s, SSM_GROUPS, SSM_GROUP_DIM)
    dt = jnp.exp(log_dt.astype(f32))[:, None]
    ar = a_re.astype(f32)
    ai = a_im.astype(f32)
    mag = jnp.exp(ar * dt)
    ang = ai * dt
    abar_re = mag * jnp.cos(ang)
    abar_im = mag * jnp.sin(ang)
    den = ar * ar + ai * ai
    nr = abar_re - 1.0
    ni = abar_im
    k_re = (nr * ar + ni * ai) / den
    k_im = (ni * ar - nr * ai) / den
    br = b_re.astype(f32)
    bi = b_im.astype(f32)
    bb_re = k_re[..., None] * br - k_im[..., None] * bi
    bb_im = k_re[..., None] * bi + k_im[..., None] * br
    bu_re = jnp.einsum('bsgh,gph->bsgp', u32, bb_re)
    bu_im = jnp.einsum('bsgh,gph->bsgp', u32, bb_im)
    shape = bu_re.shape
    _, _, x_re, x_im = lax.associative_scan(
        _scan_combine,
        (jnp.broadcast_to(abar_re, shape), jnp.broadcast_to(abar_im, shape), bu_re, bu_im),
        axis=1)
    y = (jnp.einsum('bsgp,ghp->bsgh', x_re, c_re.astype(f32))
         - jnp.einsum('bsgp,ghp->bsgh', x_im, c_im.astype(f32)))
    y = y.reshape(bsz, s, SSM_WIDTH) + d_skip.astype(f32) * u32.reshape(bsz, s, SSM_WIDTH)
    gv = jnp.einsum('bsc,ce->bse', y, w_glu.astype(f32))
    val, gate = jnp.split(gv, 2, axis=-1)
    return (val * jax.nn.sigmoid(gate)).astype(u.dtype)


def _even_mixer(h, w_in, pool_w, pool_scale, log_dt, a_re, a_im, b_re, b_im,
                c_re, c_im, d_skip, w_glu, w_out):
    proj = jnp.einsum('bsd,de->bse', h, w_in)
    u_pool = proj[..., :POOL_WIDTH]
    u_ssm = proj[..., POOL_WIDTH:MIX_WIDTH]
    z = proj[..., MIX_WIDTH:]
    y_pool = _multiscale_pool(u_pool, pool_w, pool_scale)
    y_ssm = _s5(u_ssm, log_dt, a_re, a_im, b_re, b_im, c_re, c_im, d_skip, w_glu)
    y = jnp.concatenate([y_pool, y_ssm], axis=-1) * jax.nn.silu(z)
    return jnp.einsum('bse,ed->bsd', y, w_out)


def _odd_mixer(h, w_in, conv_w, conv_b, ln_g, ln_b, w_out):
    proj = jnp.einsum('bsd,de->bse', h, w_in)
    val = proj[..., :CONV_CHANNELS]
    gt = proj[..., CONV_CHANNELS:2 * CONV_CHANNELS]
    z = proj[..., 2 * CONV_CHANNELS:]
    g = (val * jax.nn.sigmoid(gt)).astype(jnp.float32)
    kern = conv_w.astype(jnp.float32)[:, None, :]
    c = lax.conv_general_dilated(
        g, kern, window_strides=(1,), padding=[(CONV_KERNEL - 1, 0)],
        dimension_numbers=('NWC', 'WIO', 'NWC'), feature_group_count=CONV_CHANNELS)
    c = c + conv_b.astype(jnp.float32)
    mu = jnp.mean(c, axis=-1, keepdims=True)
    cc = c - mu
    var = jnp.mean(cc * cc, axis=-1, keepdims=True)
    c = cc * lax.rsqrt(var + LN_EPS) * ln_g.astype(jnp.float32) + ln_b.astype(jnp.float32)
    y = (jax.nn.silu(c) * jax.nn.silu(z.astype(jnp.float32))).astype(h.dtype)
    return jnp.einsum('bse,ed->bsd', y, w_out)


def setup_inputs(seed: int = 0) -> dict:
    key = jax.random.key(seed)
    ks = jax.random.split(key, 24)
    f32 = jnp.float32

    def nrm(k, shape, scale):
        return jax.random.normal(k, shape, f32) * scale

    n_idx = jnp.arange(SSM_STATE, dtype=f32)
    a_im0 = jnp.broadcast_to(math.pi * n_idx, (N_EVEN, SSM_GROUPS, SSM_STATE))
    return {
        "x": jax.random.normal(ks[0], (BATCH, SEQ, D_MODEL), f32),
        "even_norm": 1.0 + nrm(ks[1], (N_EVEN, D_MODEL), 0.05),
        "even_w_in": nrm(ks[2], (N_EVEN, D_MODEL, 2 * MIX_WIDTH), D_MODEL ** -0.5),
        "pool_w": nrm(ks[3], (N_EVEN, POOL_GROUPS, POOL_GROUP_DIM, POOL_GROUP_DIM), POOL_GROUP_DIM ** -0.5),
        "pool_scale": 1.0 + nrm(ks[4], (N_EVEN, POOL_WIDTH), 0.05),
        "ssm_log_dt": jax.random.uniform(ks[5], (N_EVEN, SSM_GROUPS), f32,
                                         math.log(DT_MIN), math.log(DT_MAX)),
        "ssm_a_re": -0.5 * jnp.exp(nrm(ks[6], (N_EVEN, SSM_GROUPS, SSM_STATE), 0.05)),
        "ssm_a_im": a_im0 + nrm(ks[7], (N_EVEN, SSM_GROUPS, SSM_STATE), 0.01),
        "ssm_b_re": nrm(ks[8], (N_EVEN, SSM_GROUPS, SSM_STATE, SSM_GROUP_DIM), (2 * SSM_GROUP_DIM) ** -0.5),
        "ssm_b_im": nrm(ks[9], (N_EVEN, SSM_GROUPS, SSM_STATE, SSM_GROUP_DIM), (2 * SSM_GROUP_DIM) ** -0.5),
        "ssm_c_re": nrm(ks[10], (N_EVEN, SSM_GROUPS, SSM_GROUP_DIM, SSM_STATE), SSM_STATE ** -0.5),
        "ssm_c_im": nrm(ks[11], (N_EVEN, SSM_GROUPS, SSM_GROUP_DIM, SSM_STATE), SSM_STATE ** -0.5),
        "ssm_d": nrm(ks[12], (N_EVEN, SSM_WIDTH), 1.0),
        "ssm_w_glu": nrm(ks[13], (N_EVEN, SSM_WIDTH, 2 * SSM_WIDTH), SSM_WIDTH ** -0.5),
        "even_w_out": nrm(ks[14], (N_EVEN, MIX_WIDTH, D_MODEL), MIX_WIDTH ** -0.5),
        "odd_norm": 1.0 + nrm(ks[15], (N_ODD, D_MODEL), 0.05),
        "odd_w_in": nrm(ks[16], (N_ODD, D_MODEL, 3 * CONV_CHANNELS), D_MODEL ** -0.5),
        "conv_w": nrm(ks[17], (N_ODD, CONV_KERNEL, CONV_CHANNELS), CONV_KERNEL ** -0.5),
        "conv_b": nrm(ks[18], (N_ODD, CONV_CHANNELS), 0.02),
        "conv_ln_g": 1.0 + nrm(ks[19], (N_ODD, CONV_CHANNELS), 0.05),
        "conv_ln_b": nrm(ks[20], (N_ODD, CONV_CHANNELS), 0.02),
        "odd_w_out": nrm(ks[21], (N_ODD, CONV_CHANNELS, D_MODEL), CONV_CHANNELS ** -0.5),
        "final_norm": 1.0 + nrm(ks[22], (D_MODEL,), 0.05),
    }


def reference(x, even_norm, even_w_in, pool_w, pool_scale, ssm_log_dt, ssm_a_re, ssm_a_im,
              ssm_b_re, ssm_b_im, ssm_c_re, ssm_c_im, ssm_d, ssm_w_glu, even_w_out,
              odd_norm, odd_w_in, conv_w, conv_b, conv_ln_g, conv_ln_b, odd_w_out, final_norm):
    for i in range(DEPTH):
        j = i // 2
        if i % 2 == 0:
            h = _rmsnorm(x, even_norm[j])
            x = x + _even_mixer(h, even_w_in[j], pool_w[j], pool_scale[j], ssm_log_dt[j],
                                ssm_a_re[j], ssm_a_im[j], ssm_b_re[j], ssm_b_im[j],
                                ssm_c_re[j], ssm_c_im[j], ssm_d[j], ssm_w_glu[j], even_w_out[j])
        else:
            h = _rmsnorm(x, odd_norm[j])
            x = x + _odd_mixer(h, odd_w_in[j], conv_w[j], conv_b[j], conv_ln_g[j],
                               conv_ln_b[j], odd_w_out[j])
    return _rmsnorm(x, final_norm)
```

```python
import functools

import jax
import jax.numpy as jnp
from jax import lax
from jax.experimental import pallas as pl
from jax.experimental.pallas import tpu as pltpu

F32 = jnp.float32
BF16 = jnp.bfloat16

RMS_EPS = 1e-6
LN_EPS = 1e-5
POOL_WINDOWS = (2, 4, 8, 16)
POOL_HALO = 16
SSM_GROUP_DIM = 16
SSM_STATE = 64
CONV_KERNEL = 31
CONV_HALO = 32
LANES = 128
CHUNK = 16
GROUPS_PER_SLAB = LANES // SSM_GROUP_DIM
SLAB_STATE = GROUPS_PER_SLAB * SSM_STATE
VMEM_LIMIT_BYTES = 56 * 1024 * 1024


def _sigmoid(v):
    return 1.0 / (1.0 + jnp.exp(-v))


def _silu(v):
    return v * _sigmoid(v)


def _rmsnorm(v, gain):
    ms = jnp.mean(v * v, axis=-1, keepdims=True)
    return v * lax.rsqrt(ms + RMS_EPS) * gain


def _front_kernel(x_ref, g_ref, w_ref, pw_ref, ps_ref, yzp_ref, szh_ref, u_ref, pbuf):
    t = pl.program_id(1)
    rows = x_ref.shape[1]
    pool_width = pbuf.shape[1]
    n_slab = u_ref.shape[0]

    h = _rmsnorm(x_ref[0], g_ref[...]).astype(BF16)
    proj = jnp.dot(h, w_ref[...], preferred_element_type=F32)
    mix = proj.shape[1] // 2
    u_pool = proj[:, :pool_width]
    sz = _silu(proj[:, mix:])

    for q in range(n_slab):
        lo = pool_width + q * LANES
        u_ref[q, 0] = proj[:, lo:lo + LANES].astype(BF16)
    szh_ref[0] = sz[:, pool_width:].astype(BF16)

    @pl.when(t == 0)
    def _():
        pbuf[0:POOL_HALO, :] = jnp.zeros((POOL_HALO, pool_width), F32)

    pbuf[POOL_HALO:, :] = u_pool
    pos = t * rows + lax.broadcasted_iota(jnp.int32, (rows + POOL_HALO, 1), 0) - (POOL_HALO - 1)
    for gi, win in enumerate(POOL_WINDOWS):
        sl = slice(gi * LANES, (gi + 1) * LANES)
        ext = pbuf[:, sl]
        acc = ext
        shift = 1
        while shift < win:
            acc = acc + pltpu.roll(acc, shift, 0)
            shift *= 2
        inv = 1.0 / jnp.clip(pos, 1, win).astype(F32)
        pooled = (acc * inv - ext)[POOL_HALO:, :]
        y = jnp.dot(pooled.astype(BF16), pw_ref[gi], preferred_element_type=F32) * ps_ref[:, sl]
        yzp_ref[0, :, sl] = (y * sz[:, sl]).astype(BF16)
    pbuf[0:POOL_HALO, :] = u_pool[rows - POOL_HALO:, :]


def _front_call(x, gain, w_in, pool_w, pool_scale, *, rows):
    b, s, d = x.shape
    pool_width = pool_scale.shape[1]
    mix = w_in.shape[1] // 2
    n_slab = (mix - pool_width) // LANES
    grid = (b, s // rows)
    const2 = lambda i, j: (0, 0)
    return pl.pallas_call(
        _front_kernel,
        grid=grid,
        in_specs=[
            pl.BlockSpec((1, rows, d), lambda i, j: (i, j, 0)),
            pl.BlockSpec((1, d), const2),
            pl.BlockSpec(w_in.shape, const2),
            pl.BlockSpec(pool_w.shape, lambda i, j: (0, 0, 0)),
            pl.BlockSpec((1, pool_width), const2),
        ],
        out_specs=[
            pl.BlockSpec((1, rows, pool_width), lambda i, j: (i, j, 0)),
            pl.BlockSpec((1, rows, mix - pool_width), lambda i, j: (i, j, 0)),
            pl.BlockSpec((n_slab, 1, rows, LANES), lambda i, j: (0, i, j, 0)),
        ],
        out_shape=[
            jax.ShapeDtypeStruct((b, s, pool_width), BF16),
            jax.ShapeDtypeStruct((b, s, mix - pool_width), BF16),
            jax.ShapeDtypeStruct((n_slab, b, s, LANES), BF16),
        ],
        scratch_shapes=[pltpu.VMEM((rows + POOL_HALO, pool_width), F32)],
        compiler_params=pltpu.CompilerParams(
            dimension_semantics=("arbitrary", "arbitrary"), vmem_limit_bytes=VMEM_LIMIT_BYTES),
        name="front",
    )(x, gain, w_in, pool_w, pool_scale)


def _ssm_prep_kernel(ldt_ref, ar_ref, ai_ref, br_ref, bi_ref, cr_ref, ci_ref, d_ref,
                     toe_ref, wst_ref, wout_ref, al_ref):
    ns = SLAB_STATE
    dt = jnp.exp(ldt_ref[0])
    a_re, a_im = ar_ref[0], ai_ref[0]
    lam, ang = a_re * dt, a_im * dt
    n_pow = CHUNK + 1
    n_rows = -(-n_pow // 8) * 8
    step = lax.broadcasted_iota(jnp.int32, (n_rows, ns), 0).astype(F32)
    mag = jnp.exp(step * lam)
    pw_re = mag * jnp.cos(step * ang)
    pw_im = mag * jnp.sin(step * ang)

    den = a_re * a_re + a_im * a_im
    nr, ni = pw_re[1:2] - 1.0, pw_im[1:2]
    k_re = (nr * a_re + ni * a_im) / den
    k_im = (ni * a_re - nr * a_im) / den

    row_g = lax.broadcasted_iota(jnp.int32, (LANES, ns), 0) // SSM_GROUP_DIM
    lane_g = lax.broadcasted_iota(jnp.int32, (LANES, ns), 1) // SSM_STATE
    same = row_g == lane_g
    b_re, b_im = br_ref[0], bi_ref[0]
    bb_re = jnp.where(same, k_re * b_re - k_im * b_im, 0.0)
    bb_im = jnp.where(same, k_re * b_im + k_im * b_re, 0.0)
    c_re = jnp.where(same, cr_ref[0], 0.0)
    c_im = jnp.where(same, ci_ref[0], 0.0)
    c_cat = jnp.concatenate([c_re, -c_im], axis=1)

    eye = (lax.broadcasted_iota(jnp.int32, (LANES, LANES), 0)
           == lax.broadcasted_iota(jnp.int32, (LANES, LANES), 1))
    zero_blk = jnp.zeros((LANES, LANES), BF16)
    for n in range(CHUNK):
        p_re, p_im = pw_re[n:n + 1], pw_im[n:n + 1]
        e_cat = jnp.concatenate([p_re * bb_re - p_im * bb_im, p_re * bb_im + p_im * bb_re], axis=1)
        s = CHUNK - 1 - n
        wst_ref[0, s * LANES:(s + 1) * LANES, :] = e_cat.astype(BF16)
        k_blk = lax.dot_general(e_cat, c_cat, (((1,), (1,)), ((), ())),
                                precision=lax.Precision.HIGHEST, preferred_element_type=F32)
        if n == 0:
            k_blk = k_blk + jnp.where(eye, d_ref[0], 0.0)
        k_bf = k_blk.astype(BF16)
        for s0 in range(CHUNK):
            t0 = s0 + n
            if t0 < CHUNK:
                toe_ref[0, s0 * LANES:(s0 + 1) * LANES, t0 * LANES:(t0 + 1) * LANES] = k_bf
            if n > 0:
                t1 = s0 - n
                if t1 >= 0:
                    toe_ref[0, s0 * LANES:(s0 + 1) * LANES, t1 * LANES:(t1 + 1) * LANES] = zero_blk
    for t0 in range(CHUNK):
        p_re, p_im = pw_re[t0 + 1:t0 + 2], pw_im[t0 + 1:t0 + 2]
        f_cat = jnp.concatenate([c_re * p_re - c_im * p_im, -(c_re * p_im + c_im * p_re)], axis=1)
        wout_ref[0, t0 * LANES:(t0 + 1) * LANES, :] = f_cat.astype(BF16)
    al_ref[0] = jnp.concatenate([pw_re[CHUNK:CHUNK + 1], pw_im[CHUNK:CHUNK + 1]], axis=1)


def _ssm_prep_call(ldt, a_re, a_im, b_re, b_im, c_re, c_im, d_skip):
    n_slab = ldt.shape[0]
    ns = SLAB_STATE
    cw = CHUNK * LANES
    vec = pl.BlockSpec((1, 1, ns), lambda q: (q, 0, 0))
    mat = pl.BlockSpec((1, LANES, ns), lambda q: (q, 0, 0))
    return pl.pallas_call(
        _ssm_prep_kernel,
        grid=(n_slab,),
        in_specs=[vec, vec, vec, mat, mat, mat, mat, pl.BlockSpec((1, 1, LANES), lambda q: (q, 0, 0))],
        out_specs=[
            pl.BlockSpec((1, cw, cw), lambda q: (q, 0, 0)),
            pl.BlockSpec((1, cw, 2 * ns), lambda q: (q, 0, 0)),
            pl.BlockSpec((1, cw, 2 * ns), lambda q: (q, 0, 0)),
            pl.BlockSpec((1, 1, 2 * ns), lambda q: (q, 0, 0)),
        ],
        out_shape=[
            jax.ShapeDtypeStruct((n_slab, cw, cw), BF16),
            jax.ShapeDtypeStruct((n_slab, cw, 2 * ns), BF16),
            jax.ShapeDtypeStruct((n_slab, cw, 2 * ns), BF16),
            jax.ShapeDtypeStruct((n_slab, 1, 2 * ns), F32),
        ],
        compiler_params=pltpu.CompilerParams(
            dimension_semantics=("arbitrary",), vmem_limit_bytes=VMEM_LIMIT_BYTES),
        name="ssm_prep",
    )(ldt, a_re, a_im, b_re, b_im, c_re, c_im, d_skip)


def _ssm_state_kernel(u_ref, wst_ref, al_ref, xp_ref, *, chunks_per_seq):
    ns = SLAB_STATE
    s_all = jnp.dot(u_ref[0], wst_ref[0], preferred_element_type=F32)
    x_re, x_im = s_all[:, :ns], s_all[:, ns:]
    m = s_all.shape[0]
    idx = lax.broadcasted_iota(jnp.int32, (m, 1), 0) % chunks_per_seq
    p_re, p_im = al_ref[0, :, :ns], al_ref[0, :, ns:]
    shift = 1
    while shift < chunks_per_seq:
        live = idx >= shift
        y_re = jnp.where(live, pltpu.roll(x_re, shift, 0), 0.0)
        y_im = jnp.where(live, pltpu.roll(x_im, shift, 0), 0.0)
        x_re, x_im = x_re + p_re * y_re - p_im * y_im, x_im + p_re * y_im + p_im * y_re
        p_re, p_im = p_re * p_re - p_im * p_im, 2.0 * p_re * p_im
        shift *= 2
    first = idx >= 1
    xp_ref[0, :, :ns] = jnp.where(first, pltpu.roll(x_re, 1, 0), 0.0).astype(BF16)
    xp_ref[0, :, ns:] = jnp.where(first, pltpu.roll(x_im, 1, 0), 0.0).astype(BF16)


def _ssm_state_call(u_flat, wst, al, *, chunks_per_seq):
    n_slab, m, cw = u_flat.shape
    ns2 = wst.shape[2]
    return pl.pallas_call(
        functools.partial(_ssm_state_kernel, chunks_per_seq=chunks_per_seq),
        grid=(n_slab,),
        in_specs=[
            pl.BlockSpec((1, m, cw), lambda q: (q, 0, 0)),
            pl.BlockSpec((1, cw, ns2), lambda q: (q, 0, 0)),
            pl.BlockSpec((1, 1, ns2), lambda q: (q, 0, 0)),
        ],
        out_specs=pl.BlockSpec((1, m, ns2), lambda q: (q, 0, 0)),
        out_shape=jax.ShapeDtypeStruct((n_slab, m, ns2), BF16),
        compiler_params=pltpu.CompilerParams(
            dimension_semantics=("arbitrary",), vmem_limit_bytes=VMEM_LIMIT_BYTES),
        name="ssm_state",
    )(u_flat, wst, al)


def _ssm_out_kernel(u_ref, toe_ref, xp_ref, wout_ref, y_ref):
    y = jnp.dot(u_ref[0], toe_ref[0], preferred_element_type=F32)
    y = y + lax.dot_general(xp_ref[0], wout_ref[0], (((1,), (1,)), ((), ())),
                            preferred_element_type=F32)
    y_ref[0] = y.astype(BF16)


def _ssm_out_call(u_flat, toe, xp, wout, *, tile):
    n_slab, m, cw = u_flat.shape
    ns2 = xp.shape[2]
    return pl.pallas_call(
        _ssm_out_kernel,
        grid=(n_slab, cw // tile),
        in_specs=[
            pl.BlockSpec((1, m, cw), lambda q, j: (q, 0, 0)),
            pl.BlockSpec((1, cw, tile), lambda q, j: (q, 0, j)),
            pl.BlockSpec((1, m, ns2), lambda q, j: (q, 0, 0)),
            pl.BlockSpec((1, tile, ns2), lambda q, j: (q, j, 0)),
        ],
        out_specs=pl.BlockSpec((1, m, tile), lambda q, j: (q, 0, j)),
        out_shape=jax.ShapeDtypeStruct((n_slab, m, cw), BF16),
        compiler_params=pltpu.CompilerParams(
            dimension_semantics=("arbitrary", "arbitrary"), vmem_limit_bytes=VMEM_LIMIT_BYTES),
        name="ssm_out",
    )(u_flat, toe, xp, wout)


def _back_kernel(x_ref, yzp_ref, szh_ref, ys_ref, wglu_ref, wo0_ref, n1_ref, wi1_ref, cw_ref, cb_ref,
                 lg_ref, lb_ref, wo1_ref, fn_ref, o_ref, gbuf):
    t = pl.program_id(1)
    rows = x_ref.shape[1]
    n_slab = ys_ref.shape[0]
    ch = gbuf.shape[1]

    y_ssm = jnp.concatenate([ys_ref[q, 0] for q in range(n_slab)], axis=1)
    gv = jnp.dot(y_ssm, wglu_ref[...], preferred_element_type=F32)
    half = gv.shape[1] // 2
    y_s5 = gv[:, :half] * _sigmoid(gv[:, half:])
    yz_ssm = (y_s5 * szh_ref[0].astype(F32)).astype(BF16)
    y0 = jnp.concatenate([yzp_ref[0], yz_ssm], axis=1)
    x1 = x_ref[0] + jnp.dot(y0, wo0_ref[...], preferred_element_type=F32)

    h1 = _rmsnorm(x1, n1_ref[...]).astype(BF16)
    proj = jnp.dot(h1, wi1_ref[...], preferred_element_type=F32)
    glu = proj[:, :ch] * _sigmoid(proj[:, ch:2 * ch])
    sz = _silu(proj[:, 2 * ch:])

    @pl.when(t == 0)
    def _():
        gbuf[0:CONV_HALO, :] = jnp.zeros((CONV_HALO, ch), F32)

    gbuf[CONV_HALO:, :] = glu
    base = CONV_HALO - (CONV_KERNEL - 1)
    acc = jnp.zeros((rows, ch), F32) + cb_ref[...]
    for k in range(CONV_KERNEL):
        acc = acc + cw_ref[k:k + 1, :] * gbuf[base + k:base + k + rows, :]
    gbuf[0:CONV_HALO, :] = glu[rows - CONV_HALO:, :]

    mu = jnp.mean(acc, axis=-1, keepdims=True)
    cc = acc - mu
    var = jnp.mean(cc * cc, axis=-1, keepdims=True)
    cn = cc * lax.rsqrt(var + LN_EPS) * lg_ref[...] + lb_ref[...]
    y1 = (_silu(cn) * sz).astype(BF16)
    x2 = x1 + jnp.dot(y1, wo1_ref[...], preferred_element_type=F32)
    o_ref[0] = _rmsnorm(x2, fn_ref[...])


def _back_call(x, yzp, szh, ys, wglu, wo0, n1, wi1, cw, cb, lg, lb, wo1, fn, *, rows):
    b, s, d = x.shape
    n_slab = ys.shape[0]
    grid = (b, s // rows)
    const2 = lambda i, j: (0, 0)
    full = lambda a: pl.BlockSpec(a.shape, const2)
    return pl.pallas_call(
        _back_kernel,
        grid=grid,
        in_specs=[
            pl.BlockSpec((1, rows, d), lambda i, j: (i, j, 0)),
            pl.BlockSpec((1, rows, yzp.shape[2]), lambda i, j: (i, j, 0)),
            pl.BlockSpec((1, rows, szh.shape[2]), lambda i, j: (i, j, 0)),
            pl.BlockSpec((n_slab, 1, rows, LANES), lambda i, j: (0, i, j, 0)),
            full(wglu), full(wo0), full(n1), full(wi1), full(cw), full(cb), full(lg), full(lb),
            full(wo1), full(fn),
        ],
        out_specs=pl.BlockSpec((1, rows, d), lambda i, j: (i, j, 0)),
        out_shape=jax.ShapeDtypeStruct((b, s, d), x.dtype),
        scratch_shapes=[pltpu.VMEM((rows + CONV_HALO, cw.shape[1]), F32)],
        compiler_params=pltpu.CompilerParams(
            dimension_semantics=("arbitrary", "arbitrary"), vmem_limit_bytes=VMEM_LIMIT_BYTES),
        name="back",
    )(x, yzp, szh, ys, wglu, wo0, n1, wi1, cw, cb, lg, lb, wo1, fn)


def _slab_lanes(v, n_slab):
    return v.reshape(n_slab, 1, -1)


def _slab_rows(m, n_slab):
    g, h, p = m.shape
    return jnp.tile(m.reshape(n_slab, (g // n_slab) * h, p), (1, 1, g // n_slab))


def _forward(x, even_norm, even_w_in, pool_w, pool_scale, ssm_log_dt, ssm_a_re, ssm_a_im, ssm_b_re,
             ssm_b_im, ssm_c_re, ssm_c_im, ssm_d, ssm_w_glu, even_w_out, odd_norm, odd_w_in, conv_w,
             conv_b, conv_ln_g, conv_ln_b, odd_w_out, final_norm, *, front_rows, back_rows):
    b, s, d = x.shape
    groups = ssm_log_dt.shape[1]
    n_slab = groups // GROUPS_PER_SLAB

    yzp, szh, u = _front_call(
        x, even_norm[0][None, :], even_w_in[0].astype(BF16), pool_w[0].astype(BF16),
        pool_scale[0][None, :], rows=front_rows)

    toe, wst, wout, al = _ssm_prep_call(
        _slab_lanes(jnp.repeat(ssm_log_dt[0], SSM_STATE), n_slab),
        _slab_lanes(ssm_a_re[0], n_slab), _slab_lanes(ssm_a_im[0], n_slab),
        _slab_rows(jnp.swapaxes(ssm_b_re[0], 1, 2), n_slab),
        _slab_rows(jnp.swapaxes(ssm_b_im[0], 1, 2), n_slab),
        _slab_rows(ssm_c_re[0], n_slab), _slab_rows(ssm_c_im[0], n_slab),
        ssm_d[0].reshape(n_slab, 1, LANES))

    m = b * s // CHUNK
    u_flat = u.reshape(n_slab, m, CHUNK * LANES)
    xp = _ssm_state_call(u_flat, wst, al, chunks_per_seq=s // CHUNK)
    y_flat = _ssm_out_call(u_flat, toe, xp, wout, tile=2 * LANES)
    ys = y_flat.reshape(n_slab, b, s, LANES)

    return _back_call(
        x, yzp, szh, ys, ssm_w_glu[0].astype(BF16), even_w_out[0].astype(BF16), odd_norm[0][None, :],
        odd_w_in[0].astype(BF16), conv_w[0], conv_b[0][None, :], conv_ln_g[0][None, :],
        conv_ln_b[0][None, :], odd_w_out[0].astype(BF16), final_norm[None, :], rows=back_rows)


def kernel(x, even_norm, even_w_in, pool_w, pool_scale, ssm_log_dt, ssm_a_re, ssm_a_im, ssm_b_re, ssm_b_im, ssm_c_re, ssm_c_im, ssm_d, ssm_w_glu, even_w_out, odd_norm, odd_w_in, conv_w, conv_b, conv_ln_g, conv_ln_b, odd_w_out, final_norm):
    return _forward(x, even_norm, even_w_in, pool_w, pool_scale, ssm_log_dt, ssm_a_re, ssm_a_im,
                    ssm_b_re, ssm_b_im, ssm_c_re, ssm_c_im, ssm_d, ssm_w_glu, even_w_out, odd_norm,
                    odd_w_in, conv_w, conv_b, conv_ln_g, conv_ln_b, odd_w_out, final_norm,
                    front_rows=512, back_rows=256)
```

```python
import functools

import jax
import jax.numpy as jnp
from jax import lax
from jax.experimental import pallas as pl
from jax.experimental.pallas import tpu as pltpu

F32 = jnp.float32
BF16 = jnp.bfloat16

RMS_EPS = 1e-6
LN_EPS = 1e-5
POOL_WINDOWS = (2, 4, 8, 16)
POOL_HALO = 16
SSM_GROUP_DIM = 16
SSM_STATE = 64
CONV_KERNEL = 31
CONV_HALO = 32
LANES = 128
SUBLANES = 8
CONV_ROW_BLOCK = 64
CHUNK = 16
GROUPS_PER_SLAB = LANES // SSM_GROUP_DIM
SLAB_STATE = GROUPS_PER_SLAB * SSM_STATE
VMEM_LIMIT_BYTES = 56 * 1024 * 1024


def _sigmoid(v):
    return 1.0 / (1.0 + jnp.exp(-v))


def _silu(v):
    return v * _sigmoid(v)


def _rmsnorm(v, gain):
    ms = jnp.mean(v * v, axis=-1, keepdims=True)
    return v * lax.rsqrt(ms + RMS_EPS) * gain


def _front_kernel(x_ref, g_ref, w_ref, pw_ref, ps_ref, yzp_ref, szh_ref, u_ref, pbuf, ubuf):
    t = pl.program_id(1)
    rows = x_ref.shape[1]
    pool_width = pbuf.shape[1]
    n_slab = u_ref.shape[0]
    chunks = rows // CHUNK

    h = _rmsnorm(x_ref[0], g_ref[...]).astype(BF16)
    proj = jnp.dot(h, w_ref[...], preferred_element_type=F32)
    mix = proj.shape[1] // 2
    u_pool = proj[:, :pool_width]
    sz = _silu(proj[:, mix:])

    for q in range(n_slab):
        lo = pool_width + q * LANES
        ubuf[q] = proj[:, lo:lo + LANES]
        for st in range(CHUNK):
            u_ref[q, :, st * LANES:(st + 1) * LANES] = (
                ubuf[q, pl.ds(st, chunks, stride=CHUNK), :].astype(BF16))
    szh_ref[0] = sz[:, pool_width:].astype(BF16)

    @pl.when(t == 0)
    def _():
        pbuf[0:POOL_HALO, :] = jnp.zeros((POOL_HALO, pool_width), F32)

    pbuf[POOL_HALO:, :] = u_pool
    pos = t * rows + lax.broadcasted_iota(jnp.int32, (rows + POOL_HALO, 1), 0) - (POOL_HALO - 1)
    for gi, win in enumerate(POOL_WINDOWS):
        sl = slice(gi * LANES, (gi + 1) * LANES)
        ext = pbuf[:, sl]
        acc = ext
        shift = 1
        while shift < win:
            acc = acc + pltpu.roll(acc, shift, 0)
            shift *= 2
        inv = 1.0 / jnp.clip(pos, 1, win).astype(F32)
        pooled = (acc * inv - ext)[POOL_HALO:, :]
        y = jnp.dot(pooled.astype(BF16), pw_ref[gi], preferred_element_type=F32) * ps_ref[:, sl]
        yzp_ref[0, :, sl] = (y * sz[:, sl]).astype(BF16)
    pbuf[0:POOL_HALO, :] = u_pool[rows - POOL_HALO:, :]


def _front_call(x, gain, w_in, pool_w, pool_scale, *, rows):
    b, s, d = x.shape
    pool_width = pool_scale.shape[1]
    mix = w_in.shape[1] // 2
    n_slab = (mix - pool_width) // LANES
    grid = (b, s // rows)
    const2 = lambda i, j: (0, 0)
    return pl.pallas_call(
        _front_kernel,
        grid=grid,
        in_specs=[
            pl.BlockSpec((1, rows, d), lambda i, j: (i, j, 0)),
            pl.BlockSpec((1, d), const2),
            pl.BlockSpec(w_in.shape, const2),
            pl.BlockSpec(pool_w.shape, lambda i, j: (0, 0, 0)),
            pl.BlockSpec((1, pool_width), const2),
        ],
        out_specs=[
            pl.BlockSpec((1, rows, pool_width), lambda i, j: (i, j, 0)),
            pl.BlockSpec((1, rows, mix - pool_width), lambda i, j: (i, j, 0)),
            pl.BlockSpec((n_slab, rows // CHUNK, CHUNK * LANES), lambda i, j, n=s // rows: (0, i * n + j, 0)),
        ],
        out_shape=[
            jax.ShapeDtypeStruct((b, s, pool_width), BF16),
            jax.ShapeDtypeStruct((b, s, mix - pool_width), BF16),
            jax.ShapeDtypeStruct((n_slab, b * s // CHUNK, CHUNK * LANES), BF16),
        ],
        scratch_shapes=[pltpu.VMEM((rows + POOL_HALO, pool_width), F32),
                        pltpu.VMEM((n_slab, rows, LANES), F32)],
        compiler_params=pltpu.CompilerParams(
            dimension_semantics=("arbitrary", "arbitrary"), vmem_limit_bytes=VMEM_LIMIT_BYTES),
        name="front",
    )(x, gain, w_in, pool_w, pool_scale)


def _ssm_prep_kernel(ldt_ref, ar_ref, ai_ref, br_ref, bi_ref, cr_ref, ci_ref, d_ref,
                     toe_ref, wst_ref, wout_ref, al_ref):
    ns = SLAB_STATE
    dt = jnp.exp(ldt_ref[0])
    a_re, a_im = ar_ref[0], ai_ref[0]
    lam, ang = a_re * dt, a_im * dt
    n_pow = CHUNK + 1
    n_rows = -(-n_pow // 8) * 8
    step = lax.broadcasted_iota(jnp.int32, (n_rows, ns), 0).astype(F32)
    mag = jnp.exp(step * lam)
    pw_re = mag * jnp.cos(step * ang)
    pw_im = mag * jnp.sin(step * ang)

    den = a_re * a_re + a_im * a_im
    nr, ni = pw_re[1:2] - 1.0, pw_im[1:2]
    k_re = (nr * a_re + ni * a_im) / den
    k_im = (ni * a_re - nr * a_im) / den

    row_g = lax.broadcasted_iota(jnp.int32, (LANES, ns), 0) // SSM_GROUP_DIM
    lane_g = lax.broadcasted_iota(jnp.int32, (LANES, ns), 1) // SSM_STATE
    same = row_g == lane_g
    b_re, b_im = br_ref[0], bi_ref[0]
    bb_re = jnp.where(same, k_re * b_re - k_im * b_im, 0.0)
    bb_im = jnp.where(same, k_re * b_im + k_im * b_re, 0.0)
    c_re = jnp.where(same, cr_ref[0], 0.0)
    c_im = jnp.where(same, ci_ref[0], 0.0)
    c_bf = jnp.concatenate([c_re, -c_im], axis=1).astype(BF16)

    eye = (lax.broadcasted_iota(jnp.int32, (LANES, LANES), 0)
           == lax.broadcasted_iota(jnp.int32, (LANES, LANES), 1))
    zero_blk = jnp.zeros((LANES, LANES), BF16)
    for n in range(CHUNK):
        p_re, p_im = pw_re[n:n + 1], pw_im[n:n + 1]
        e_cat = jnp.concatenate([p_re * bb_re - p_im * bb_im, p_re * bb_im + p_im * bb_re], axis=1)
        s = CHUNK - 1 - n
        e_bf = e_cat.astype(BF16)
        wst_ref[0, s * LANES:(s + 1) * LANES, :] = e_bf
        k_blk = lax.dot_general(e_bf, c_bf, (((1,), (1,)), ((), ())), preferred_element_type=F32)
        if n == 0:
            k_blk = k_blk + jnp.where(eye, d_ref[0], 0.0)
        k_bf = k_blk.astype(BF16)
        for s0 in range(CHUNK):
            t0 = s0 + n
            if t0 < CHUNK:
                toe_ref[0, s0 * LANES:(s0 + 1) * LANES, t0 * LANES:(t0 + 1) * LANES] = k_bf
            if n > 0:
                t1 = s0 - n
                if t1 >= 0:
                    toe_ref[0, s0 * LANES:(s0 + 1) * LANES, t1 * LANES:(t1 + 1) * LANES] = zero_blk
    for t0 in range(CHUNK):
        p_re, p_im = pw_re[t0 + 1:t0 + 2], pw_im[t0 + 1:t0 + 2]
        f_cat = jnp.concatenate([c_re * p_re - c_im * p_im, -(c_re * p_im + c_im * p_re)], axis=1)
        wout_ref[0, t0 * LANES:(t0 + 1) * LANES, :] = f_cat.astype(BF16)
    al_ref[0] = jnp.concatenate([pw_re[CHUNK:CHUNK + 1], pw_im[CHUNK:CHUNK + 1]], axis=1)


def _ssm_prep_call(ldt, a_re, a_im, b_re, b_im, c_re, c_im, d_skip):
    n_slab = ldt.shape[0]
    ns = SLAB_STATE
    cw = CHUNK * LANES
    vec = pl.BlockSpec((1, 1, ns), lambda q: (q, 0, 0))
    mat = pl.BlockSpec((1, LANES, ns), lambda q: (q, 0, 0))
    return pl.pallas_call(
        _ssm_prep_kernel,
        grid=(n_slab,),
        in_specs=[vec, vec, vec, mat, mat, mat, mat, pl.BlockSpec((1, 1, LANES), lambda q: (q, 0, 0))],
        out_specs=[
            pl.BlockSpec((1, cw, cw), lambda q: (q, 0, 0)),
            pl.BlockSpec((1, cw, 2 * ns), lambda q: (q, 0, 0)),
            pl.BlockSpec((1, cw, 2 * ns), lambda q: (q, 0, 0)),
            pl.BlockSpec((1, 1, 2 * ns), lambda q: (q, 0, 0)),
        ],
        out_shape=[
            jax.ShapeDtypeStruct((n_slab, cw, cw), BF16),
            jax.ShapeDtypeStruct((n_slab, cw, 2 * ns), BF16),
            jax.ShapeDtypeStruct((n_slab, cw, 2 * ns), BF16),
            jax.ShapeDtypeStruct((n_slab, 1, 2 * ns), F32),
        ],
        compiler_params=pltpu.CompilerParams(
            dimension_semantics=("arbitrary",), vmem_limit_bytes=VMEM_LIMIT_BYTES),
        name="ssm_prep",
    )(ldt, a_re, a_im, b_re, b_im, c_re, c_im, d_skip)


def _ssm_state_kernel(u_ref, wst_ref, al_ref, xp_ref, *, chunks_per_seq):
    ns = SLAB_STATE
    s_all = jnp.dot(u_ref[0], wst_ref[0], preferred_element_type=F32)
    x_re, x_im = s_all[:, :ns], s_all[:, ns:]
    m = s_all.shape[0]
    idx = lax.broadcasted_iota(jnp.int32, (m, 1), 0) % chunks_per_seq
    p_re, p_im = al_ref[0, :, :ns], al_ref[0, :, ns:]
    shift = 1
    while shift < chunks_per_seq:
        live = idx >= shift
        y_re = jnp.where(live, pltpu.roll(x_re, shift, 0), 0.0)
        y_im = jnp.where(live, pltpu.roll(x_im, shift, 0), 0.0)
        x_re, x_im = x_re + p_re * y_re - p_im * y_im, x_im + p_re * y_im + p_im * y_re
        p_re, p_im = p_re * p_re - p_im * p_im, 2.0 * p_re * p_im
        shift *= 2
    first = idx >= 1
    xp_ref[0, :, :ns] = jnp.where(first, pltpu.roll(x_re, 1, 0), 0.0).astype(BF16)
    xp_ref[0, :, ns:] = jnp.where(first, pltpu.roll(x_im, 1, 0), 0.0).astype(BF16)


def _ssm_state_call(u_flat, wst, al, *, chunks_per_seq):
    n_slab, m, cw = u_flat.shape
    ns2 = wst.shape[2]
    return pl.pallas_call(
        functools.partial(_ssm_state_kernel, chunks_per_seq=chunks_per_seq),
        grid=(n_slab,),
        in_specs=[
            pl.BlockSpec((1, m, cw), lambda q: (q, 0, 0)),
            pl.BlockSpec((1, cw, ns2), lambda q: (q, 0, 0)),
            pl.BlockSpec((1, 1, ns2), lambda q: (q, 0, 0)),
        ],
        out_specs=pl.BlockSpec((1, m, ns2), lambda q: (q, 0, 0)),
        out_shape=jax.ShapeDtypeStruct((n_slab, m, ns2), BF16),
        compiler_params=pltpu.CompilerParams(
            dimension_semantics=("arbitrary",), vmem_limit_bytes=VMEM_LIMIT_BYTES),
        name="ssm_state",
    )(u_flat, wst, al)


def _ssm_out_kernel(u_ref, toe_ref, xp_ref, wout_ref, y_ref, *, tile):
    cw = toe_ref.shape[1]
    for j in range(cw // tile):
        hi = (j + 1) * tile
        y = jnp.dot(u_ref[0, :, :hi], toe_ref[0, :hi, j * tile:hi], preferred_element_type=F32)
        y = y + lax.dot_general(xp_ref[0], wout_ref[0, j * tile:hi, :], (((1,), (1,)), ((), ())),
                                preferred_element_type=F32)
        y_ref[0, :, j * tile:hi] = y.astype(BF16)


def _ssm_out_call(u_flat, toe, xp, wout, *, tile):
    n_slab, m, cw = u_flat.shape
    ns2 = xp.shape[2]
    slab = lambda a: pl.BlockSpec((1,) + a.shape[1:], lambda q: (q, 0, 0))
    return pl.pallas_call(
        functools.partial(_ssm_out_kernel, tile=tile),
        grid=(n_slab,),
        in_specs=[slab(u_flat), slab(toe), slab(xp), slab(wout)],
        out_specs=pl.BlockSpec((1, m, cw), lambda q: (q, 0, 0)),
        out_shape=jax.ShapeDtypeStruct((n_slab, m, cw), BF16),
        compiler_params=pltpu.CompilerParams(
            dimension_semantics=("arbitrary",), vmem_limit_bytes=VMEM_LIMIT_BYTES),
        name="ssm_out",
    )(u_flat, toe, xp, wout)


def _causal_dwconv(gbuf, cw_ref, cb_ref, cbuf, rows):
    ch = gbuf.shape[1]
    base = CONV_HALO - (CONV_KERNEL - 1)
    for i in range(rows // CONV_ROW_BLOCK):
        r0 = i * CONV_ROW_BLOCK
        for j in range(ch // LANES):
            ln = slice(j * LANES, (j + 1) * LANES)
            acc = jnp.zeros((CONV_ROW_BLOCK, LANES), F32) + cb_ref[:, ln]
            for phase in range(SUBLANES):
                taps = [k for k in range(CONV_KERNEL) if (base + k) % SUBLANES == phase]
                span = CONV_ROW_BLOCK + (SUBLANES if phase else 0)
                part = None
                for k in taps:
                    lo = r0 + base + k - phase
                    term = cw_ref[k:k + 1, ln] * gbuf[lo:lo + span, ln]
                    part = term if part is None else part + term
                acc = acc + part[phase:phase + CONV_ROW_BLOCK, :]
            cbuf[r0:r0 + CONV_ROW_BLOCK, ln] = acc


def _back_kernel(x_ref, yzp_ref, szh_ref, ys_ref, wglu_ref, wo0_ref, n1_ref, wi1_ref, cw_ref, cb_ref,
                 lg_ref, lb_ref, wo1_ref, fn_ref, o_ref, gbuf, cbuf, ybuf):
    t = pl.program_id(1)
    rows = x_ref.shape[1]
    n_slab = ys_ref.shape[0]
    ch = gbuf.shape[1]
    chunks = rows // CHUNK

    for q in range(n_slab):
        for st in range(CHUNK):
            ybuf[q, pl.ds(st, chunks, stride=CHUNK), :] = (
                ys_ref[q, :, st * LANES:(st + 1) * LANES].astype(F32))
    y_ssm = jnp.concatenate([ybuf[q] for q in range(n_slab)], axis=1).astype(BF16)
    gv = jnp.dot(y_ssm, wglu_ref[...], preferred_element_type=F32)
    half = gv.shape[1] // 2
    y_s5 = gv[:, :half] * _sigmoid(gv[:, half:])
    yz_ssm = (y_s5 * szh_ref[0].astype(F32)).astype(BF16)
    y0 = jnp.concatenate([yzp_ref[0], yz_ssm], axis=1)
    x1 = x_ref[0] + jnp.dot(y0, wo0_ref[...], preferred_element_type=F32)

    h1 = _rmsnorm(x1, n1_ref[...]).astype(BF16)
    proj = jnp.dot(h1, wi1_ref[...], preferred_element_type=F32)
    glu = proj[:, :ch] * _sigmoid(proj[:, ch:2 * ch])
    sz = _silu(proj[:, 2 * ch:])

    @pl.when(t == 0)
    def _():
        gbuf[0:CONV_HALO, :] = jnp.zeros((CONV_HALO, ch), F32)

    gbuf[CONV_HALO:, :] = glu
    _causal_dwconv(gbuf, cw_ref, cb_ref, cbuf, rows)
    gbuf[0:CONV_HALO, :] = glu[rows - CONV_HALO:, :]

    acc = cbuf[...]
    mu = jnp.mean(acc, axis=-1, keepdims=True)
    cc = acc - mu
    var = jnp.mean(cc * cc, axis=-1, keepdims=True)
    cn = cc * lax.rsqrt(var + LN_EPS) * lg_ref[...] + lb_ref[...]
    y1 = (_silu(cn) * sz).astype(BF16)
    x2 = x1 + jnp.dot(y1, wo1_ref[...], preferred_element_type=F32)
    o_ref[0] = _rmsnorm(x2, fn_ref[...])


def _back_call(x, yzp, szh, ys, wglu, wo0, n1, wi1, cw, cb, lg, lb, wo1, fn, *, rows):
    b, s, d = x.shape
    n_slab = ys.shape[0]
    grid = (b, s // rows)
    const2 = lambda i, j: (0, 0)
    full = lambda a: pl.BlockSpec(a.shape, const2)
    return pl.pallas_call(
        _back_kernel,
        grid=grid,
        in_specs=[
            pl.BlockSpec((1, rows, d), lambda i, j: (i, j, 0)),
            pl.BlockSpec((1, rows, yzp.shape[2]), lambda i, j: (i, j, 0)),
            pl.BlockSpec((1, rows, szh.shape[2]), lambda i, j: (i, j, 0)),
            pl.BlockSpec((n_slab, rows // CHUNK, CHUNK * LANES), lambda i, j, n=s // rows: (0, i * n + j, 0)),
            full(wglu), full(wo0), full(n1), full(wi1), full(cw), full(cb), full(lg), full(lb),
            full(wo1), full(fn),
        ],
        out_specs=pl.BlockSpec((1, rows, d), lambda i, j: (i, j, 0)),
        out_shape=jax.ShapeDtypeStruct((b, s, d), x.dtype),
        scratch_shapes=[pltpu.VMEM((rows + CONV_HALO, cw.shape[1]), F32),
                        pltpu.VMEM((rows, cw.shape[1]), F32),
                        pltpu.VMEM((n_slab, rows, LANES), F32)],
        compiler_params=pltpu.CompilerParams(
            dimension_semantics=("arbitrary", "arbitrary"), vmem_limit_bytes=VMEM_LIMIT_BYTES),
        name="back",
    )(x, yzp, szh, ys, wglu, wo0, n1, wi1, cw, cb, lg, lb, wo1, fn)


def _slab_lanes(v, n_slab):
    return v.reshape(n_slab, 1, -1)


def _slab_rows(m, n_slab):
    g, h, p = m.shape
    return jnp.tile(m.reshape(n_slab, (g // n_slab) * h, p), (1, 1, g // n_slab))


def _forward(x, even_norm, even_w_in, pool_w, pool_scale, ssm_log_dt, ssm_a_re, ssm_a_im, ssm_b_re,
             ssm_b_im, ssm_c_re, ssm_c_im, ssm_d, ssm_w_glu, even_w_out, odd_norm, odd_w_in, conv_w,
             conv_b, conv_ln_g, conv_ln_b, odd_w_out, final_norm, *, front_rows, back_rows):
    b, s, d = x.shape
    groups = ssm_log_dt.shape[1]
    n_slab = groups // GROUPS_PER_SLAB

    yzp, szh, u = _front_call(
        x, even_norm[0][None, :], even_w_in[0].astype(BF16), pool_w[0].astype(BF16),
        pool_scale[0][None, :], rows=front_rows)

    toe, wst, wout, al = _ssm_prep_call(
        _slab_lanes(jnp.repeat(ssm_log_dt[0], SSM_STATE), n_slab),
        _slab_lanes(ssm_a_re[0], n_slab), _slab_lanes(ssm_a_im[0], n_slab),
        _slab_rows(jnp.swapaxes(ssm_b_re[0], 1, 2), n_slab),
        _slab_rows(jnp.swapaxes(ssm_b_im[0], 1, 2), n_slab),
        _slab_rows(ssm_c_re[0], n_slab), _slab_rows(ssm_c_im[0], n_slab),
        ssm_d[0].reshape(n_slab, 1, LANES))

    xp = _ssm_state_call(u, wst, al, chunks_per_seq=s // CHUNK)
    ys = _ssm_out_call(u, toe, xp, wout, tile=2 * LANES)

    return _back_call(
        x, yzp, szh, ys, ssm_w_glu[0].astype(BF16), even_w_out[0].astype(BF16), odd_norm[0][None, :],
        odd_w_in[0].astype(BF16), conv_w[0], conv_b[0][None, :], conv_ln_g[0][None, :],
        conv_ln_b[0][None, :], odd_w_out[0].astype(BF16), final_norm[None, :], rows=back_rows)


def kernel(x, even_norm, even_w_in, pool_w, pool_scale, ssm_log_dt, ssm_a_re, ssm_a_im, ssm_b_re, ssm_b_im, ssm_c_re, ssm_c_im, ssm_d, ssm_w_glu, even_w_out, odd_norm, odd_w_in, conv_w, conv_b, conv_ln_g, conv_ln_b, odd_w_out, final_norm):
    return _forward(x, even_norm, even_w_in, pool_w, pool_scale, ssm_log_dt, ssm_a_re, ssm_a_im,
                    ssm_b_re, ssm_b_im, ssm_c_re, ssm_c_im, ssm_d, ssm_w_glu, even_w_out, odd_norm,
                    odd_w_in, conv_w, conv_b, conv_ln_g, conv_ln_b, odd_w_out, final_norm,
                    front_rows=512, back_rows=256)
```

```python
import functools

import jax
import jax.numpy as jnp
from jax import lax
from jax.experimental import pallas as pl
from jax.experimental.pallas import tpu as pltpu

F32 = jnp.float32
BF16 = jnp.bfloat16

RMS_EPS = 1e-6
LN_EPS = 1e-5
POOL_WINDOWS = (2, 4, 8, 16)
POOL_HALO = 16
SSM_GROUP_DIM = 16
SSM_STATE = 64
CONV_KERNEL = 31
CONV_HALO = 32
LANES = 128
CONV_ROW_BLOCK = 64
CHUNK = 16
GROUPS_PER_SLAB = LANES // SSM_GROUP_DIM
SLAB_STATE = GROUPS_PER_SLAB * SSM_STATE
VMEM_LIMIT_BYTES = 56 * 1024 * 1024


def _sigmoid(v):
    return 1.0 / (1.0 + jnp.exp(-v))


def _silu(v):
    return v * _sigmoid(v)


def _rmsnorm(v, gain):
    ms = jnp.mean(v * v, axis=-1, keepdims=True)
    return v * lax.rsqrt(ms + RMS_EPS) * gain


def _front_kernel(x_ref, g_ref, w_ref, pw_ref, ps_ref, yzp_ref, szh_ref, u_ref, pbuf, ubuf):
    t = pl.program_id(1)
    rows = x_ref.shape[1]
    pool_width = pbuf.shape[1]
    n_slab = u_ref.shape[0]
    chunks = rows // CHUNK

    h = _rmsnorm(x_ref[0], g_ref[...]).astype(BF16)
    proj = jnp.dot(h, w_ref[...], preferred_element_type=F32)
    mix = proj.shape[1] // 2
    u_pool = proj[:, :pool_width]
    sz = _silu(proj[:, mix:])

    for q in range(n_slab):
        lo = pool_width + q * LANES
        ubuf[q] = proj[:, lo:lo + LANES]
        for st in range(CHUNK):
            u_ref[q, :, st * LANES:(st + 1) * LANES] = (
                ubuf[q, pl.ds(st, chunks, stride=CHUNK), :].astype(BF16))
    szh_ref[0] = sz[:, pool_width:].astype(BF16)

    @pl.when(t == 0)
    def _():
        pbuf[0:POOL_HALO, :] = jnp.zeros((POOL_HALO, pool_width), F32)

    pbuf[POOL_HALO:, :] = u_pool
    pos = t * rows + lax.broadcasted_iota(jnp.int32, (rows + POOL_HALO, 1), 0) - (POOL_HALO - 1)
    for gi, win in enumerate(POOL_WINDOWS):
        sl = slice(gi * LANES, (gi + 1) * LANES)
        ext = pbuf[:, sl]
        acc = ext
        shift = 1
        while shift < win:
            acc = acc + pltpu.roll(acc, shift, 0)
            shift *= 2
        inv = 1.0 / jnp.clip(pos, 1, win).astype(F32)
        pooled = (acc * inv - ext)[POOL_HALO:, :]
        y = jnp.dot(pooled.astype(BF16), pw_ref[gi], preferred_element_type=F32) * ps_ref[:, sl]
        yzp_ref[0, :, sl] = (y * sz[:, sl]).astype(BF16)
    pbuf[0:POOL_HALO, :] = u_pool[rows - POOL_HALO:, :]


def _front_call(x, gain, w_in, pool_w, pool_scale, *, rows):
    b, s, d = x.shape
    pool_width = pool_scale.shape[1]
    mix = w_in.shape[1] // 2
    n_slab = (mix - pool_width) // LANES
    grid = (b, s // rows)
    const2 = lambda i, j: (0, 0)
    return pl.pallas_call(
        _front_kernel,
        grid=grid,
        in_specs=[
            pl.BlockSpec((1, rows, d), lambda i, j: (i, j, 0)),
            pl.BlockSpec((1, d), const2),
            pl.BlockSpec(w_in.shape, const2),
            pl.BlockSpec(pool_w.shape, lambda i, j: (0, 0, 0)),
            pl.BlockSpec((1, pool_width), const2),
        ],
        out_specs=[
            pl.BlockSpec((1, rows, pool_width), lambda i, j: (i, j, 0)),
            pl.BlockSpec((1, rows, mix - pool_width), lambda i, j: (i, j, 0)),
            pl.BlockSpec((n_slab, rows // CHUNK, CHUNK * LANES), lambda i, j, n=s // rows: (0, i * n + j, 0)),
        ],
        out_shape=[
            jax.ShapeDtypeStruct((b, s, pool_width), BF16),
            jax.ShapeDtypeStruct((b, s, mix - pool_width), BF16),
            jax.ShapeDtypeStruct((n_slab, b * s // CHUNK, CHUNK * LANES), BF16),
        ],
        scratch_shapes=[pltpu.VMEM((rows + POOL_HALO, pool_width), F32),
                        pltpu.VMEM((n_slab, rows, LANES), F32)],
        compiler_params=pltpu.CompilerParams(
            dimension_semantics=("arbitrary", "arbitrary"), vmem_limit_bytes=VMEM_LIMIT_BYTES),
        name="front",
    )(x, gain, w_in, pool_w, pool_scale)


def _ssm_prep_kernel(ldt_ref, ar_ref, ai_ref, br_ref, bi_ref, cr_ref, ci_ref, d_ref,
                     toe_ref, wst_ref, wout_ref, al_ref):
    ns = SLAB_STATE
    dt = jnp.exp(ldt_ref[0])
    a_re, a_im = ar_ref[0], ai_ref[0]
    lam, ang = a_re * dt, a_im * dt
    n_pow = CHUNK + 1
    n_rows = -(-n_pow // 8) * 8
    step = lax.broadcasted_iota(jnp.int32, (n_rows, ns), 0).astype(F32)
    mag = jnp.exp(step * lam)
    pw_re = mag * jnp.cos(step * ang)
    pw_im = mag * jnp.sin(step * ang)

    den = a_re * a_re + a_im * a_im
    nr, ni = pw_re[1:2] - 1.0, pw_im[1:2]
    k_re = (nr * a_re + ni * a_im) / den
    k_im = (ni * a_re - nr * a_im) / den

    row_g = lax.broadcasted_iota(jnp.int32, (LANES, ns), 0) // SSM_GROUP_DIM
    lane_g = lax.broadcasted_iota(jnp.int32, (LANES, ns), 1) // SSM_STATE
    same = row_g == lane_g
    b_re, b_im = br_ref[0], bi_ref[0]
    bb_re = jnp.where(same, k_re * b_re - k_im * b_im, 0.0)
    bb_im = jnp.where(same, k_re * b_im + k_im * b_re, 0.0)
    c_re = jnp.where(same, cr_ref[0], 0.0)
    c_im = jnp.where(same, ci_ref[0], 0.0)
    c_bf = jnp.concatenate([c_re, -c_im], axis=1).astype(BF16)

    eye = (lax.broadcasted_iota(jnp.int32, (LANES, LANES), 0)
           == lax.broadcasted_iota(jnp.int32, (LANES, LANES), 1))
    zero_blk = jnp.zeros((LANES, LANES), BF16)
    for n in range(CHUNK):
        p_re, p_im = pw_re[n:n + 1], pw_im[n:n + 1]
        e_cat = jnp.concatenate([p_re * bb_re - p_im * bb_im, p_re * bb_im + p_im * bb_re], axis=1)
        s = CHUNK - 1 - n
        e_bf = e_cat.astype(BF16)
        wst_ref[0, s * LANES:(s + 1) * LANES, :] = e_bf
        k_blk = lax.dot_general(e_bf, c_bf, (((1,), (1,)), ((), ())), preferred_element_type=F32)
        if n == 0:
            k_blk = k_blk + jnp.where(eye, d_ref[0], 0.0)
        k_bf = k_blk.astype(BF16)
        for s0 in range(CHUNK):
            t0 = s0 + n
            if t0 < CHUNK:
                toe_ref[0, s0 * LANES:(s0 + 1) * LANES, t0 * LANES:(t0 + 1) * LANES] = k_bf
            if n > 0:
                t1 = s0 - n
                if t1 >= 0:
                    toe_ref[0, s0 * LANES:(s0 + 1) * LANES, t1 * LANES:(t1 + 1) * LANES] = zero_blk
    for t0 in range(CHUNK):
        p_re, p_im = pw_re[t0 + 1:t0 + 2], pw_im[t0 + 1:t0 + 2]
        f_cat = jnp.concatenate([c_re * p_re - c_im * p_im, -(c_re * p_im + c_im * p_re)], axis=1)
        wout_ref[0, t0 * LANES:(t0 + 1) * LANES, :] = f_cat.astype(BF16)
    al_ref[0] = jnp.concatenate([pw_re[CHUNK:CHUNK + 1], pw_im[CHUNK:CHUNK + 1]], axis=1)


def _ssm_prep_call(ldt, a_re, a_im, b_re, b_im, c_re, c_im, d_skip):
    n_slab = ldt.shape[0]
    ns = SLAB_STATE
    cw = CHUNK * LANES
    vec = pl.BlockSpec((1, 1, ns), lambda q: (q, 0, 0))
    mat = pl.BlockSpec((1, LANES, ns), lambda q: (q, 0, 0))
    return pl.pallas_call(
        _ssm_prep_kernel,
        grid=(n_slab,),
        in_specs=[vec, vec, vec, mat, mat, mat, mat, pl.BlockSpec((1, 1, LANES), lambda q: (q, 0, 0))],
        out_specs=[
            pl.BlockSpec((1, cw, cw), lambda q: (q, 0, 0)),
            pl.BlockSpec((1, cw, 2 * ns), lambda q: (q, 0, 0)),
            pl.BlockSpec((1, cw, 2 * ns), lambda q: (q, 0, 0)),
            pl.BlockSpec((1, 1, 2 * ns), lambda q: (q, 0, 0)),
        ],
        out_shape=[
            jax.ShapeDtypeStruct((n_slab, cw, cw), BF16),
            jax.ShapeDtypeStruct((n_slab, cw, 2 * ns), BF16),
            jax.ShapeDtypeStruct((n_slab, cw, 2 * ns), BF16),
            jax.ShapeDtypeStruct((n_slab, 1, 2 * ns), F32),
        ],
        compiler_params=pltpu.CompilerParams(
            dimension_semantics=("arbitrary",), vmem_limit_bytes=VMEM_LIMIT_BYTES),
        name="ssm_prep",
    )(ldt, a_re, a_im, b_re, b_im, c_re, c_im, d_skip)


def _ssm_state_kernel(u_ref, wst_ref, al_ref, xp_ref, *, chunks_per_seq):
    ns = SLAB_STATE
    s_all = jnp.dot(u_ref[0], wst_ref[0], preferred_element_type=F32)
    x_re, x_im = s_all[:, :ns], s_all[:, ns:]
    m = s_all.shape[0]
    idx = lax.broadcasted_iota(jnp.int32, (m, 1), 0) % chunks_per_seq
    p_re, p_im = al_ref[0, :, :ns], al_ref[0, :, ns:]
    shift = 1
    while shift < chunks_per_seq:
        live = idx >= shift
        y_re = jnp.where(live, pltpu.roll(x_re, shift, 0), 0.0)
        y_im = jnp.where(live, pltpu.roll(x_im, shift, 0), 0.0)
        x_re, x_im = x_re + p_re * y_re - p_im * y_im, x_im + p_re * y_im + p_im * y_re
        p_re, p_im = p_re * p_re - p_im * p_im, 2.0 * p_re * p_im
        shift *= 2
    first = idx >= 1
    xp_ref[0, :, :ns] = jnp.where(first, pltpu.roll(x_re, 1, 0), 0.0).astype(BF16)
    xp_ref[0, :, ns:] = jnp.where(first, pltpu.roll(x_im, 1, 0), 0.0).astype(BF16)


def _ssm_state_call(u_flat, wst, al, *, chunks_per_seq):
    n_slab, m, cw = u_flat.shape
    ns2 = wst.shape[2]
    return pl.pallas_call(
        functools.partial(_ssm_state_kernel, chunks_per_seq=chunks_per_seq),
        grid=(n_slab,),
        in_specs=[
            pl.BlockSpec((1, m, cw), lambda q: (q, 0, 0)),
            pl.BlockSpec((1, cw, ns2), lambda q: (q, 0, 0)),
            pl.BlockSpec((1, 1, ns2), lambda q: (q, 0, 0)),
        ],
        out_specs=pl.BlockSpec((1, m, ns2), lambda q: (q, 0, 0)),
        out_shape=jax.ShapeDtypeStruct((n_slab, m, ns2), BF16),
        compiler_params=pltpu.CompilerParams(
            dimension_semantics=("arbitrary",), vmem_limit_bytes=VMEM_LIMIT_BYTES),
        name="ssm_state",
    )(u_flat, wst, al)


def _ssm_out_kernel(u_ref, toe_ref, xp_ref, wout_ref, y_ref, *, tile):
    cw = toe_ref.shape[1]
    for j in range(cw // tile):
        hi = (j + 1) * tile
        y = jnp.dot(u_ref[0, :, :hi], toe_ref[0, :hi, j * tile:hi], preferred_element_type=F32)
        y = y + lax.dot_general(xp_ref[0], wout_ref[0, j * tile:hi, :], (((1,), (1,)), ((), ())),
                                preferred_element_type=F32)
        y_ref[0, :, j * tile:hi] = y.astype(BF16)


def _ssm_out_call(u_flat, toe, xp, wout, *, tile):
    n_slab, m, cw = u_flat.shape
    ns2 = xp.shape[2]
    slab = lambda a: pl.BlockSpec((1,) + a.shape[1:], lambda q: (q, 0, 0))
    return pl.pallas_call(
        functools.partial(_ssm_out_kernel, tile=tile),
        grid=(n_slab,),
        in_specs=[slab(u_flat), slab(toe), slab(xp), slab(wout)],
        out_specs=pl.BlockSpec((1, m, cw), lambda q: (q, 0, 0)),
        out_shape=jax.ShapeDtypeStruct((n_slab, m, cw), BF16),
        compiler_params=pltpu.CompilerParams(
            dimension_semantics=("arbitrary",), vmem_limit_bytes=VMEM_LIMIT_BYTES),
        name="ssm_out",
    )(u_flat, toe, xp, wout)


def _causal_dwconv(gbuf, cw_ref, cb_ref, cbuf, rows):
    base = CONV_HALO - (CONV_KERNEL - 1)
    half = CONV_ROW_BLOCK // 2
    for j in range(gbuf.shape[0]):
        ln = slice(j * LANES, (j + 1) * LANES)
        for i in range(rows // CONV_ROW_BLOCK):
            r0 = i * CONV_ROW_BLOCK
            for parity in range(2):
                acc = jnp.zeros((half, LANES), F32) + cb_ref[:, ln]
                for k in range(CONV_KERNEL):
                    taps = gbuf[j, pl.ds(r0 + parity + base + k, half, stride=2), :]
                    acc = acc + cw_ref[k:k + 1, ln] * taps
                cbuf[j, pl.ds(r0 + parity, half, stride=2), :] = acc


def _back_kernel(x_ref, yzp_ref, szh_ref, ys_ref, wglu_ref, wo0_ref, n1_ref, wi1_ref, cw_ref, cb_ref,
                 lg_ref, lb_ref, wo1_ref, fn_ref, o_ref, gbuf, cbuf, ybuf):
    t = pl.program_id(1)
    rows = x_ref.shape[1]
    n_slab = ys_ref.shape[0]
    c_slab = gbuf.shape[0]
    ch = c_slab * LANES
    chunks = rows // CHUNK

    for q in range(n_slab):
        for st in range(CHUNK):
            ybuf[q, pl.ds(st, chunks, stride=CHUNK), :] = (
                ys_ref[q, :, st * LANES:(st + 1) * LANES].astype(F32))
    y_ssm = jnp.concatenate([ybuf[q] for q in range(n_slab)], axis=1).astype(BF16)
    gv = jnp.dot(y_ssm, wglu_ref[...], preferred_element_type=F32)
    half = gv.shape[1] // 2
    y_s5 = gv[:, :half] * _sigmoid(gv[:, half:])
    yz_ssm = (y_s5 * szh_ref[0].astype(F32)).astype(BF16)
    y0 = jnp.concatenate([yzp_ref[0], yz_ssm], axis=1)
    x1 = x_ref[0] + jnp.dot(y0, wo0_ref[...], preferred_element_type=F32)

    h1 = _rmsnorm(x1, n1_ref[...]).astype(BF16)
    proj = jnp.dot(h1, wi1_ref[...], preferred_element_type=F32)
    glu = proj[:, :ch] * _sigmoid(proj[:, ch:2 * ch])
    sz = _silu(proj[:, 2 * ch:])

    @pl.when(t == 0)
    def _():
        gbuf[:, 0:CONV_HALO, :] = jnp.zeros((c_slab, CONV_HALO, LANES), F32)

    for j in range(c_slab):
        gbuf[j, CONV_HALO:, :] = glu[:, j * LANES:(j + 1) * LANES]
    _causal_dwconv(gbuf, cw_ref, cb_ref, cbuf, rows)
    for j in range(c_slab):
        gbuf[j, 0:CONV_HALO, :] = glu[rows - CONV_HALO:, j * LANES:(j + 1) * LANES]

    acc = jnp.concatenate([cbuf[j] for j in range(c_slab)], axis=1)
    mu = jnp.mean(acc, axis=-1, keepdims=True)
    cc = acc - mu
    var = jnp.mean(cc * cc, axis=-1, keepdims=True)
    cn = cc * lax.rsqrt(var + LN_EPS) * lg_ref[...] + lb_ref[...]
    y1 = (_silu(cn) * sz).astype(BF16)
    x2 = x1 + jnp.dot(y1, wo1_ref[...], preferred_element_type=F32)
    o_ref[0] = _rmsnorm(x2, fn_ref[...])


def _back_call(x, yzp, szh, ys, wglu, wo0, n1, wi1, cw, cb, lg, lb, wo1, fn, *, rows):
    b, s, d = x.shape
    n_slab = ys.shape[0]
    grid = (b, s // rows)
    const2 = lambda i, j: (0, 0)
    full = lambda a: pl.BlockSpec(a.shape, const2)
    return pl.pallas_call(
        _back_kernel,
        grid=grid,
        in_specs=[
            pl.BlockSpec((1, rows, d), lambda i, j: (i, j, 0)),
            pl.BlockSpec((1, rows, yzp.shape[2]), lambda i, j: (i, j, 0)),
            pl.BlockSpec((1, rows, szh.shape[2]), lambda i, j: (i, j, 0)),
            pl.BlockSpec((n_slab, rows // CHUNK, CHUNK * LANES), lambda i, j, n=s // rows: (0, i * n + j, 0)),
            full(wglu), full(wo0), full(n1), full(wi1), full(cw), full(cb), full(lg), full(lb),
            full(wo1), full(fn),
        ],
        out_specs=pl.BlockSpec((1, rows, d), lambda i, j: (i, j, 0)),
        out_shape=jax.ShapeDtypeStruct((b, s, d), x.dtype),
        scratch_shapes=[pltpu.VMEM((cw.shape[1] // LANES, rows + CONV_HALO, LANES), F32),
                        pltpu.VMEM((cw.shape[1] // LANES, rows, LANES), F32),
                        pltpu.VMEM((n_slab, rows, LANES), F32)],
        compiler_params=pltpu.CompilerParams(
            dimension_semantics=("arbitrary", "arbitrary"), vmem_limit_bytes=VMEM_LIMIT_BYTES),
        name="back",
    )(x, yzp, szh, ys, wglu, wo0, n1, wi1, cw, cb, lg, lb, wo1, fn)


def _slab_lanes(v, n_slab):
    return v.reshape(n_slab, 1, -1)


def _slab_rows(m, n_slab):
    g, h, p = m.shape
    return jnp.tile(m.reshape(n_slab, (g // n_slab) * h, p), (1, 1, g // n_slab))


def _forward(x, even_norm, even_w_in, pool_w, pool_scale, ssm_log_dt, ssm_a_re, ssm_a_im, ssm_b_re,
             ssm_b_im, ssm_c_re, ssm_c_im, ssm_d, ssm_w_glu, even_w_out, odd_norm, odd_w_in, conv_w,
             conv_b, conv_ln_g, conv_ln_b, odd_w_out, final_norm, *, front_rows, back_rows):
    b, s, d = x.shape
    groups = ssm_log_dt.shape[1]
    n_slab = groups // GROUPS_PER_SLAB

    yzp, szh, u = _front_call(
        x, even_norm[0][None, :], even_w_in[0].astype(BF16), pool_w[0].astype(BF16),
        pool_scale[0][None, :], rows=front_rows)

    toe, wst, wout, al = _ssm_prep_call(
        _slab_lanes(jnp.repeat(ssm_log_dt[0], SSM_STATE), n_slab),
        _slab_lanes(ssm_a_re[0], n_slab), _slab_lanes(ssm_a_im[0], n_slab),
        _slab_rows(jnp.swapaxes(ssm_b_re[0], 1, 2), n_slab),
        _slab_rows(jnp.swapaxes(ssm_b_im[0], 1, 2), n_slab),
        _slab_rows(ssm_c_re[0], n_slab), _slab_rows(ssm_c_im[0], n_slab),
        ssm_d[0].reshape(n_slab, 1, LANES))

    xp = _ssm_state_call(u, wst, al, chunks_per_seq=s // CHUNK)
    ys = _ssm_out_call(u, toe, xp, wout, tile=2 * LANES)

    return _back_call(
        x, yzp, szh, ys, ssm_w_glu[0].astype(BF16), even_w_out[0].astype(BF16), odd_norm[0][None, :],
        odd_w_in[0].astype(BF16), conv_w[0], conv_b[0][None, :], conv_ln_g[0][None, :],
        conv_ln_b[0][None, :], odd_w_out[0].astype(BF16), final_norm[None, :], rows=back_rows)


def kernel(x, even_norm, even_w_in, pool_w, pool_scale, ssm_log_dt, ssm_a_re, ssm_a_im, ssm_b_re, ssm_b_im, ssm_c_re, ssm_c_im, ssm_d, ssm_w_glu, even_w_out, odd_norm, odd_w_in, conv_w, conv_b, conv_ln_g, conv_ln_b, odd_w_out, final_norm):
    return _forward(x, even_norm, even_w_in, pool_w, pool_scale, ssm_log_dt, ssm_a_re, ssm_a_im,
                    ssm_b_re, ssm_b_im, ssm_c_re, ssm_c_im, ssm_d, ssm_w_glu, even_w_out, odd_norm,
                    odd_w_in, conv_w, conv_b, conv_ln_g, conv_ln_b, odd_w_out, final_norm,
                    front_rows=1024, back_rows=512)
```

```python
import functools

import jax
import jax.numpy as jnp
from jax import lax
from jax.experimental import pallas as pl
from jax.experimental.pallas import tpu as pltpu

F32 = jnp.float32
BF16 = jnp.bfloat16

RMS_EPS = 1e-6
LN_EPS = 1e-5
POOL_WINDOWS = (2, 4, 8, 16)
POOL_HALO = 16
SSM_GROUP_DIM = 16
SSM_STATE = 64
CONV_KERNEL = 31
CONV_HALO = 32
LANES = 128
CONV_ROW_BLOCK = 64
CHUNK = 16
GROUPS_PER_SLAB = LANES // SSM_GROUP_DIM
SLAB_STATE = GROUPS_PER_SLAB * SSM_STATE
VMEM_LIMIT_BYTES = 56 * 1024 * 1024


def _sigmoid(v):
    return 1.0 / (1.0 + jnp.exp(-v))


def _silu(v):
    return v * _sigmoid(v)


def _rmsnorm(v, gain):
    ms = jnp.mean(v * v, axis=-1, keepdims=True)
    return v * lax.rsqrt(ms + RMS_EPS) * gain


def _front_kernel(x_ref, g_ref, w_ref, pw_ref, ps_ref, yzp_ref, szh_ref, u_ref, pbuf, ubuf):
    t = pl.program_id(1)
    rows = x_ref.shape[1]
    pool_width = pbuf.shape[1]
    n_slab = u_ref.shape[0]
    chunks = rows // CHUNK

    h = _rmsnorm(x_ref[0], g_ref[...]).astype(BF16)
    proj = jnp.dot(h, w_ref[...], preferred_element_type=F32)
    mix = proj.shape[1] // 2
    u_pool = proj[:, :pool_width]
    sz = _silu(proj[:, mix:])

    for q in range(n_slab):
        lo = pool_width + q * LANES
        ubuf[q] = proj[:, lo:lo + LANES]
        for st in range(CHUNK):
            u_ref[q, :, st * LANES:(st + 1) * LANES] = (
                ubuf[q, pl.ds(st, chunks, stride=CHUNK), :].astype(BF16))
    szh_ref[0] = sz[:, pool_width:].astype(BF16)

    @pl.when(t == 0)
    def _():
        pbuf[0:POOL_HALO, :] = jnp.zeros((POOL_HALO, pool_width), F32)

    pbuf[POOL_HALO:, :] = u_pool
    pos = t * rows + lax.broadcasted_iota(jnp.int32, (rows + POOL_HALO, 1), 0) - (POOL_HALO - 1)
    for gi, win in enumerate(POOL_WINDOWS):
        sl = slice(gi * LANES, (gi + 1) * LANES)
        ext = pbuf[:, sl]
        acc = ext
        shift = 1
        while shift < win:
            acc = acc + pltpu.roll(acc, shift, 0)
            shift *= 2
        inv = 1.0 / jnp.clip(pos, 1, win).astype(F32)
        pooled = (acc * inv - ext)[POOL_HALO:, :]
        y = jnp.dot(pooled.astype(BF16), pw_ref[gi], preferred_element_type=F32) * ps_ref[:, sl]
        yzp_ref[0, :, sl] = (y * sz[:, sl]).astype(BF16)
    pbuf[0:POOL_HALO, :] = u_pool[rows - POOL_HALO:, :]


def _front_call(x, gain, w_in, pool_w, pool_scale, *, rows):
    b, s, d = x.shape
    pool_width = pool_scale.shape[1]
    mix = w_in.shape[1] // 2
    n_slab = (mix - pool_width) // LANES
    grid = (b, s // rows)
    const2 = lambda i, j: (0, 0)
    return pl.pallas_call(
        _front_kernel,
        grid=grid,
        in_specs=[
            pl.BlockSpec((1, rows, d), lambda i, j: (i, j, 0)),
            pl.BlockSpec((1, d), const2),
            pl.BlockSpec(w_in.shape, const2),
            pl.BlockSpec(pool_w.shape, lambda i, j: (0, 0, 0)),
            pl.BlockSpec((1, pool_width), const2),
        ],
        out_specs=[
            pl.BlockSpec((1, rows, pool_width), lambda i, j: (i, j, 0)),
            pl.BlockSpec((1, rows, mix - pool_width), lambda i, j: (i, j, 0)),
            pl.BlockSpec((n_slab, rows // CHUNK, CHUNK * LANES), lambda i, j, n=s // rows: (0, i * n + j, 0)),
        ],
        out_shape=[
            jax.ShapeDtypeStruct((b, s, pool_width), BF16),
            jax.ShapeDtypeStruct((b, s, mix - pool_width), BF16),
            jax.ShapeDtypeStruct((n_slab, b * s // CHUNK, CHUNK * LANES), BF16),
        ],
        scratch_shapes=[pltpu.VMEM((rows + POOL_HALO, pool_width), F32),
                        pltpu.VMEM((n_slab, rows, LANES), F32)],
        compiler_params=pltpu.CompilerParams(
            dimension_semantics=("arbitrary", "arbitrary"), vmem_limit_bytes=VMEM_LIMIT_BYTES),
        name="front",
    )(x, gain, w_in, pool_w, pool_scale)


def _ssm_tables(ldt_ref, ar_ref, ai_ref, br_ref, bi_ref, cr_ref, ci_ref, d_ref, toe, wst, wout, tile):
    ns = SLAB_STATE
    dt = jnp.exp(ldt_ref[0])
    a_re, a_im = ar_ref[0], ai_ref[0]
    lam, ang = a_re * dt, a_im * dt
    n_pow = CHUNK + 1
    n_rows = -(-n_pow // 8) * 8
    step = lax.broadcasted_iota(jnp.int32, (n_rows, ns), 0).astype(F32)
    mag = jnp.exp(step * lam)
    pw_re = mag * jnp.cos(step * ang)
    pw_im = mag * jnp.sin(step * ang)

    den = a_re * a_re + a_im * a_im
    nr, ni = pw_re[1:2] - 1.0, pw_im[1:2]
    k_re = (nr * a_re + ni * a_im) / den
    k_im = (ni * a_re - nr * a_im) / den

    row_g = lax.broadcasted_iota(jnp.int32, (LANES, ns), 0) // SSM_GROUP_DIM
    lane_g = lax.broadcasted_iota(jnp.int32, (LANES, ns), 1) // SSM_STATE
    same = row_g == lane_g
    b_re, b_im = br_ref[0], bi_ref[0]
    bb_re = jnp.where(same, k_re * b_re - k_im * b_im, 0.0)
    bb_im = jnp.where(same, k_re * b_im + k_im * b_re, 0.0)
    c_re = jnp.where(same, cr_ref[0], 0.0)
    c_im = jnp.where(same, ci_ref[0], 0.0)
    c_bf = jnp.concatenate([c_re, -c_im], axis=1).astype(BF16)

    eye = (lax.broadcasted_iota(jnp.int32, (LANES, LANES), 0)
           == lax.broadcasted_iota(jnp.int32, (LANES, LANES), 1))
    zero_blk = jnp.zeros((LANES, LANES), BF16)
    steps_per_tile = tile // LANES
    for n in range(CHUNK):
        p_re, p_im = pw_re[n:n + 1], pw_im[n:n + 1]
        e_cat = jnp.concatenate([p_re * bb_re - p_im * bb_im, p_re * bb_im + p_im * bb_re], axis=1)
        s = CHUNK - 1 - n
        e_bf = e_cat.astype(BF16)
        wst[s * LANES:(s + 1) * LANES, :] = e_bf
        k_blk = lax.dot_general(e_bf, c_bf, (((1,), (1,)), ((), ())), preferred_element_type=F32)
        if n == 0:
            k_blk = k_blk + jnp.where(eye, d_ref[0], 0.0)
        k_bf = k_blk.astype(BF16)
        for s0 in range(CHUNK - n):
            t0 = s0 + n
            toe[s0 * LANES:(s0 + 1) * LANES, t0 * LANES:(t0 + 1) * LANES] = k_bf
    for s0 in range(CHUNK):
        for t0 in range(s0 // steps_per_tile * steps_per_tile, s0):
            toe[s0 * LANES:(s0 + 1) * LANES, t0 * LANES:(t0 + 1) * LANES] = zero_blk
    for t0 in range(CHUNK):
        p_re, p_im = pw_re[t0 + 1:t0 + 2], pw_im[t0 + 1:t0 + 2]
        f_cat = jnp.concatenate([c_re * p_re - c_im * p_im, -(c_re * p_im + c_im * p_re)], axis=1)
        wout[t0 * LANES:(t0 + 1) * LANES, :] = f_cat.astype(BF16)
    return pw_re[CHUNK:CHUNK + 1], pw_im[CHUNK:CHUNK + 1]


def _ssm_kernel(ldt_ref, ar_ref, ai_ref, br_ref, bi_ref, cr_ref, ci_ref, d_ref, u_ref, y_ref,
                toe, wst, wout, *, chunks_per_seq, tile):
    ns = SLAB_STATE
    p_re, p_im = _ssm_tables(ldt_ref, ar_ref, ai_ref, br_ref, bi_ref, cr_ref, ci_ref, d_ref,
                             toe, wst, wout, tile)

    s_all = jnp.dot(u_ref[0], wst[...], preferred_element_type=F32)
    x_re, x_im = s_all[:, :ns], s_all[:, ns:]
    m = s_all.shape[0]
    idx = lax.broadcasted_iota(jnp.int32, (m, 1), 0) % chunks_per_seq
    shift = 1
    while shift < chunks_per_seq:
        live = idx >= shift
        y_re = jnp.where(live, pltpu.roll(x_re, shift, 0), 0.0)
        y_im = jnp.where(live, pltpu.roll(x_im, shift, 0), 0.0)
        x_re, x_im = x_re + p_re * y_re - p_im * y_im, x_im + p_re * y_im + p_im * y_re
        p_re, p_im = p_re * p_re - p_im * p_im, 2.0 * p_re * p_im
        shift *= 2
    first = idx >= 1
    xp = jnp.concatenate([jnp.where(first, pltpu.roll(x_re, 1, 0), 0.0),
                          jnp.where(first, pltpu.roll(x_im, 1, 0), 0.0)], axis=1).astype(BF16)

    cw = toe.shape[1]
    for j in range(cw // tile):
        hi = (j + 1) * tile
        y = jnp.dot(u_ref[0, :, :hi], toe[:hi, j * tile:hi], preferred_element_type=F32)
        y = y + lax.dot_general(xp, wout[j * tile:hi, :], (((1,), (1,)), ((), ())),
                                preferred_element_type=F32)
        y_ref[0, :, j * tile:hi] = y.astype(BF16)


def _ssm_call(ldt, a_re, a_im, b_re, b_im, c_re, c_im, d_skip, u_flat, *, chunks_per_seq, tile):
    n_slab, m, cw = u_flat.shape
    ns = SLAB_STATE
    vec = pl.BlockSpec((1, 1, ns), lambda q: (q, 0, 0))
    mat = pl.BlockSpec((1, LANES, ns), lambda q: (q, 0, 0))
    return pl.pallas_call(
        functools.partial(_ssm_kernel, chunks_per_seq=chunks_per_seq, tile=tile),
        grid=(n_slab,),
        in_specs=[vec, vec, vec, mat, mat, mat, mat, pl.BlockSpec((1, 1, LANES), lambda q: (q, 0, 0)),
                  pl.BlockSpec((1, m, cw), lambda q: (q, 0, 0))],
        out_specs=pl.BlockSpec((1, m, cw), lambda q: (q, 0, 0)),
        out_shape=jax.ShapeDtypeStruct((n_slab, m, cw), BF16),
        scratch_shapes=[pltpu.VMEM((cw, cw), BF16),
                        pltpu.VMEM((cw, 2 * ns), BF16),
                        pltpu.VMEM((cw, 2 * ns), BF16)],
        compiler_params=pltpu.CompilerParams(
            dimension_semantics=("arbitrary",), vmem_limit_bytes=VMEM_LIMIT_BYTES),
        name="ssm",
    )(ldt, a_re, a_im, b_re, b_im, c_re, c_im, d_skip, u_flat)


def _causal_dwconv(gbuf, cw_ref, cb_ref, cbuf, rows):
    base = CONV_HALO - (CONV_KERNEL - 1)
    half = CONV_ROW_BLOCK // 2
    for j in range(gbuf.shape[0]):
        ln = slice(j * LANES, (j + 1) * LANES)
        for i in range(rows // CONV_ROW_BLOCK):
            r0 = i * CONV_ROW_BLOCK
            for parity in range(2):
                acc = jnp.zeros((half, LANES), F32) + cb_ref[:, ln]
                for k in range(CONV_KERNEL):
                    taps = gbuf[j, pl.ds(r0 + parity + base + k, half, stride=2), :]
                    acc = acc + cw_ref[k:k + 1, ln] * taps
                cbuf[j, pl.ds(r0 + parity, half, stride=2), :] = acc


def _back_kernel(x_ref, yzp_ref, szh_ref, ys_ref, wglu_ref, wo0_ref, n1_ref, wi1_ref, cw_ref, cb_ref,
                 lg_ref, lb_ref, wo1_ref, fn_ref, o_ref, gbuf, cbuf, ybuf):
    t = pl.program_id(1)
    rows = x_ref.shape[1]
    n_slab = ys_ref.shape[0]
    c_slab = gbuf.shape[0]
    ch = c_slab * LANES
    chunks = rows // CHUNK

    for q in range(n_slab):
        for st in range(CHUNK):
            ybuf[q, pl.ds(st, chunks, stride=CHUNK), :] = (
                ys_ref[q, :, st * LANES:(st + 1) * LANES].astype(F32))
    y_ssm = jnp.concatenate([ybuf[q] for q in range(n_slab)], axis=1).astype(BF16)
    gv = jnp.dot(y_ssm, wglu_ref[...], preferred_element_type=F32)
    half = gv.shape[1] // 2
    y_s5 = gv[:, :half] * _sigmoid(gv[:, half:])
    yz_ssm = (y_s5 * szh_ref[0].astype(F32)).astype(BF16)
    y0 = jnp.concatenate([yzp_ref[0], yz_ssm], axis=1)
    x1 = x_ref[0] + jnp.dot(y0, wo0_ref[...], preferred_element_type=F32)

    h1 = _rmsnorm(x1, n1_ref[...]).astype(BF16)
    proj = jnp.dot(h1, wi1_ref[...], preferred_element_type=F32)
    glu = proj[:, :ch] * _sigmoid(proj[:, ch:2 * ch])
    sz = _silu(proj[:, 2 * ch:])

    @pl.when(t == 0)
    def _():
        gbuf[:, 0:CONV_HALO, :] = jnp.zeros((c_slab, CONV_HALO, LANES), F32)

    for j in range(c_slab):
        gbuf[j, CONV_HALO:, :] = glu[:, j * LANES:(j + 1) * LANES]
    _causal_dwconv(gbuf, cw_ref, cb_ref, cbuf, rows)
    for j in range(c_slab):
        gbuf[j, 0:CONV_HALO, :] = glu[rows - CONV_HALO:, j * LANES:(j + 1) * LANES]

    acc = jnp.concatenate([cbuf[j] for j in range(c_slab)], axis=1)
    mu = jnp.mean(acc, axis=-1, keepdims=True)
    cc = acc - mu
    var = jnp.mean(cc * cc, axis=-1, keepdims=True)
    cn = cc * lax.rsqrt(var + LN_EPS) * lg_ref[...] + lb_ref[...]
    y1 = (_silu(cn) * sz).astype(BF16)
    x2 = x1 + jnp.dot(y1, wo1_ref[...], preferred_element_type=F32)
    o_ref[0] = _rmsnorm(x2, fn_ref[...])


def _back_call(x, yzp, szh, ys, wglu, wo0, n1, wi1, cw, cb, lg, lb, wo1, fn, *, rows):
    b, s, d = x.shape
    n_slab = ys.shape[0]
    grid = (b, s // rows)
    const2 = lambda i, j: (0, 0)
    full = lambda a: pl.BlockSpec(a.shape, const2)
    return pl.pallas_call(
        _back_kernel,
        grid=grid,
        in_specs=[
            pl.BlockSpec((1, rows, d), lambda i, j: (i, j, 0)),
            pl.BlockSpec((1, rows, yzp.shape[2]), lambda i, j: (i, j, 0)),
            pl.BlockSpec((1, rows, szh.shape[2]), lambda i, j: (i, j, 0)),
            pl.BlockSpec((n_slab, rows // CHUNK, CHUNK * LANES), lambda i, j, n=s // rows: (0, i * n + j, 0)),
            full(wglu), full(wo0), full(n1), full(wi1), full(cw), full(cb), full(lg), full(lb),
            full(wo1), full(fn),
        ],
        out_specs=pl.BlockSpec((1, rows, d), lambda i, j: (i, j, 0)),
        out_shape=jax.ShapeDtypeStruct((b, s, d), x.dtype),
        scratch_shapes=[pltpu.VMEM((cw.shape[1] // LANES, rows + CONV_HALO, LANES), F32),
                        pltpu.VMEM((cw.shape[1] // LANES, rows, LANES), F32),
                        pltpu.VMEM((n_slab, rows, LANES), F32)],
        compiler_params=pltpu.CompilerParams(
            dimension_semantics=("arbitrary", "arbitrary"), vmem_limit_bytes=VMEM_LIMIT_BYTES),
        name="back",
    )(x, yzp, szh, ys, wglu, wo0, n1, wi1, cw, cb, lg, lb, wo1, fn)


def _slab_lanes(v, n_slab):
    return v.reshape(n_slab, 1, -1)


def _slab_rows(m, n_slab):
    g, h, p = m.shape
    return jnp.tile(m.reshape(n_slab, (g // n_slab) * h, p), (1, 1, g // n_slab))


def _forward(x, even_norm, even_w_in, pool_w, pool_scale, ssm_log_dt, ssm_a_re, ssm_a_im, ssm_b_re,
             ssm_b_im, ssm_c_re, ssm_c_im, ssm_d, ssm_w_glu, even_w_out, odd_norm, odd_w_in, conv_w,
             conv_b, conv_ln_g, conv_ln_b, odd_w_out, final_norm, *, front_rows, back_rows):
    b, s, d = x.shape
    groups = ssm_log_dt.shape[1]
    n_slab = groups // GROUPS_PER_SLAB

    yzp, szh, u = _front_call(
        x, even_norm[0][None, :], even_w_in[0].astype(BF16), pool_w[0].astype(BF16),
        pool_scale[0][None, :], rows=front_rows)

    ys = _ssm_call(
        _slab_lanes(jnp.repeat(ssm_log_dt[0], SSM_STATE), n_slab),
        _slab_lanes(ssm_a_re[0], n_slab), _slab_lanes(ssm_a_im[0], n_slab),
        _slab_rows(jnp.swapaxes(ssm_b_re[0], 1, 2), n_slab),
        _slab_rows(jnp.swapaxes(ssm_b_im[0], 1, 2), n_slab),
        _slab_rows(ssm_c_re[0], n_slab), _slab_rows(ssm_c_im[0], n_slab),
        ssm_d[0].reshape(n_slab, 1, LANES), u, chunks_per_seq=s // CHUNK, tile=2 * LANES)

    return _back_call(
        x, yzp, szh, ys, ssm_w_glu[0].astype(BF16), even_w_out[0].astype(BF16), odd_norm[0][None, :],
        odd_w_in[0].astype(BF16), conv_w[0], conv_b[0][None, :], conv_ln_g[0][None, :],
        conv_ln_b[0][None, :], odd_w_out[0].astype(BF16), final_norm[None, :], rows=back_rows)


def kernel(x, even_norm, even_w_in, pool_w, pool_scale, ssm_log_dt, ssm_a_re, ssm_a_im, ssm_b_re, ssm_b_im, ssm_c_re, ssm_c_im, ssm_d, ssm_w_glu, even_w_out, odd_norm, odd_w_in, conv_w, conv_b, conv_ln_g, conv_ln_b, odd_w_out, final_norm):
    return _forward(x, even_norm, even_w_in, pool_w, pool_scale, ssm_log_dt, ssm_a_re, ssm_a_im,
                    ssm_b_re, ssm_b_im, ssm_c_re, ssm_c_im, ssm_d, ssm_w_glu, even_w_out, odd_norm,
                    odd_w_in, conv_w, conv_b, conv_ln_g, conv_ln_b, odd_w_out, final_norm,
                    front_rows=1024, back_rows=512)
```

```python
import functools

import jax
import jax.numpy as jnp
from jax import lax
from jax.experimental import pallas as pl
from jax.experimental.pallas import tpu as pltpu

F32 = jnp.float32
BF16 = jnp.bfloat16

RMS_EPS = 1e-6
LN_EPS = 1e-5
POOL_WINDOWS = (2, 4, 8, 16)
POOL_HALO = 16
SSM_GROUP_DIM = 16
SSM_STATE = 64
CONV_KERNEL = 31
CONV_HALO = 32
LANES = 128
CONV_ROW_BLOCK = 64
CHUNK = 16
ROW_SUBTILES = 2
GROUPS_PER_SLAB = LANES // SSM_GROUP_DIM
SLAB_STATE = GROUPS_PER_SLAB * SSM_STATE
VMEM_LIMIT_BYTES = 56 * 1024 * 1024


def _sigmoid(v):
    return 1.0 / (1.0 + jnp.exp(-v))


def _silu(v):
    return v * _sigmoid(v)


def _rmsnorm(v, gain):
    ms = jnp.mean(v * v, axis=-1, keepdims=True)
    return v * lax.rsqrt(ms + RMS_EPS) * gain


def _front_kernel(x_ref, g_ref, w_ref, pw_ref, ps_ref, yzp_ref, szh_ref, u_ref, pbuf, ubuf):
    t = pl.program_id(1)
    rows = x_ref.shape[1]
    pool_width = pbuf.shape[1]
    n_slab = u_ref.shape[0]

    @pl.when(t == 0)
    def _():
        pbuf[0:POOL_HALO, :] = jnp.zeros((POOL_HALO, pool_width), F32)

    mix = w_ref.shape[1] // 2
    sub = rows // ROW_SUBTILES
    for hh in range(ROW_SUBTILES):
        r0 = hh * sub
        rs = slice(r0, r0 + sub)
        h = _rmsnorm(x_ref[0, rs, :], g_ref[...]).astype(BF16)
        proj = jnp.dot(h, w_ref[...], preferred_element_type=F32)
        sz = _silu(proj[:, mix:])

        for q in range(n_slab):
            lo = pool_width + q * LANES
            ubuf[q, rs, :] = proj[:, lo:lo + LANES]
            for st in range(CHUNK):
                u_ref[q, r0 // CHUNK:(r0 + sub) // CHUNK, st * LANES:(st + 1) * LANES] = (
                    ubuf[q, pl.ds(r0 + st, sub // CHUNK, stride=CHUNK), :].astype(BF16))
        szh_ref[0, rs, :] = sz[:, pool_width:].astype(BF16)

        pbuf[POOL_HALO + r0:POOL_HALO + r0 + sub, :] = proj[:, :pool_width]
        pos = (t * rows + r0 + lax.broadcasted_iota(jnp.int32, (sub + POOL_HALO, 1), 0)
               - (POOL_HALO - 1))
        for gi, win in enumerate(POOL_WINDOWS):
            sl = slice(gi * LANES, (gi + 1) * LANES)
            ext = pbuf[r0:r0 + sub + POOL_HALO, sl]
            acc = ext
            shift = 1
            while shift < win:
                acc = acc + pltpu.roll(acc, shift, 0)
                shift *= 2
            inv = 1.0 / jnp.clip(pos, 1, win).astype(F32)
            pooled = (acc * inv - ext)[POOL_HALO:, :]
            y = jnp.dot(pooled.astype(BF16), pw_ref[gi], preferred_element_type=F32) * ps_ref[:, sl]
            yzp_ref[0, rs, sl] = (y * sz[:, sl]).astype(BF16)
    pbuf[0:POOL_HALO, :] = pbuf[rows:rows + POOL_HALO, :]


def _front_call(x, gain, w_in, pool_w, pool_scale, *, rows):
    b, s, d = x.shape
    pool_width = pool_scale.shape[1]
    mix = w_in.shape[1] // 2
    n_slab = (mix - pool_width) // LANES
    grid = (b, s // rows)
    const2 = lambda i, j: (0, 0)
    return pl.pallas_call(
        _front_kernel,
        grid=grid,
        in_specs=[
            pl.BlockSpec((1, rows, d), lambda i, j: (i, j, 0)),
            pl.BlockSpec((1, d), const2),
            pl.BlockSpec(w_in.shape, const2),
            pl.BlockSpec(pool_w.shape, lambda i, j: (0, 0, 0)),
            pl.BlockSpec((1, pool_width), const2),
        ],
        out_specs=[
            pl.BlockSpec((1, rows, pool_width), lambda i, j: (i, j, 0)),
            pl.BlockSpec((1, rows, mix - pool_width), lambda i, j: (i, j, 0)),
            pl.BlockSpec((n_slab, rows // CHUNK, CHUNK * LANES), lambda i, j, n=s // rows: (0, i * n + j, 0)),
        ],
        out_shape=[
            jax.ShapeDtypeStruct((b, s, pool_width), BF16),
            jax.ShapeDtypeStruct((b, s, mix - pool_width), BF16),
            jax.ShapeDtypeStruct((n_slab, b * s // CHUNK, CHUNK * LANES), BF16),
        ],
        scratch_shapes=[pltpu.VMEM((rows + POOL_HALO, pool_width), F32),
                        pltpu.VMEM((n_slab, rows, LANES), F32)],
        compiler_params=pltpu.CompilerParams(
            dimension_semantics=("arbitrary", "arbitrary"), vmem_limit_bytes=VMEM_LIMIT_BYTES),
        name="front",
    )(x, gain, w_in, pool_w, pool_scale)


def _ssm_tables(ldt_ref, ar_ref, ai_ref, br_ref, bi_ref, cr_ref, ci_ref, d_ref, toe, wst, wout, tile):
    ns = SLAB_STATE
    dt = jnp.exp(ldt_ref[0])
    a_re, a_im = ar_ref[0], ai_ref[0]
    lam, ang = a_re * dt, a_im * dt
    n_pow = CHUNK + 1
    n_rows = -(-n_pow // 8) * 8
    step = lax.broadcasted_iota(jnp.int32, (n_rows, ns), 0).astype(F32)
    mag = jnp.exp(step * lam)
    pw_re = mag * jnp.cos(step * ang)
    pw_im = mag * jnp.sin(step * ang)

    den = a_re * a_re + a_im * a_im
    nr, ni = pw_re[1:2] - 1.0, pw_im[1:2]
    k_re = (nr * a_re + ni * a_im) / den
    k_im = (ni * a_re - nr * a_im) / den

    row_g = lax.broadcasted_iota(jnp.int32, (LANES, ns), 0) // SSM_GROUP_DIM
    lane_g = lax.broadcasted_iota(jnp.int32, (LANES, ns), 1) // SSM_STATE
    same = row_g == lane_g
    b_re, b_im = br_ref[0], bi_ref[0]
    bb_re = jnp.where(same, k_re * b_re - k_im * b_im, 0.0)
    bb_im = jnp.where(same, k_re * b_im + k_im * b_re, 0.0)
    c_re = jnp.where(same, cr_ref[0], 0.0)
    c_im = jnp.where(same, ci_ref[0], 0.0)
    c_bf = jnp.concatenate([c_re, -c_im], axis=1).astype(BF16)

    eye = (lax.broadcasted_iota(jnp.int32, (LANES, LANES), 0)
           == lax.broadcasted_iota(jnp.int32, (LANES, LANES), 1))
    zero_blk = jnp.zeros((LANES, LANES), BF16)
    steps_per_tile = tile // LANES
    for n in range(CHUNK):
        p_re, p_im = pw_re[n:n + 1], pw_im[n:n + 1]
        e_cat = jnp.concatenate([p_re * bb_re - p_im * bb_im, p_re * bb_im + p_im * bb_re], axis=1)
        s = CHUNK - 1 - n
        e_bf = e_cat.astype(BF16)
        wst[s * LANES:(s + 1) * LANES, :] = e_bf
        k_blk = lax.dot_general(e_bf, c_bf, (((1,), (1,)), ((), ())), preferred_element_type=F32)
        if n == 0:
            k_blk = k_blk + jnp.where(eye, d_ref[0], 0.0)
        k_bf = k_blk.astype(BF16)
        for s0 in range(CHUNK - n):
            t0 = s0 + n
            toe[s0 * LANES:(s0 + 1) * LANES, t0 * LANES:(t0 + 1) * LANES] = k_bf
    for s0 in range(CHUNK):
        for t0 in range(s0 // steps_per_tile * steps_per_tile, s0):
            toe[s0 * LANES:(s0 + 1) * LANES, t0 * LANES:(t0 + 1) * LANES] = zero_blk
    for t0 in range(CHUNK):
        p_re, p_im = pw_re[t0 + 1:t0 + 2], pw_im[t0 + 1:t0 + 2]
        f_cat = jnp.concatenate([c_re * p_re - c_im * p_im, -(c_re * p_im + c_im * p_re)], axis=1)
        wout[t0 * LANES:(t0 + 1) * LANES, :] = f_cat.astype(BF16)
    return pw_re[CHUNK:CHUNK + 1], pw_im[CHUNK:CHUNK + 1]


def _ssm_kernel(ldt_ref, ar_ref, ai_ref, br_ref, bi_ref, cr_ref, ci_ref, d_ref, u_ref, y_ref,
                toe, wst, wout, *, chunks_per_seq, tile):
    ns = SLAB_STATE
    p_re, p_im = _ssm_tables(ldt_ref, ar_ref, ai_ref, br_ref, bi_ref, cr_ref, ci_ref, d_ref,
                             toe, wst, wout, tile)

    s_all = jnp.dot(u_ref[0], wst[...], preferred_element_type=F32)
    x_re, x_im = s_all[:, :ns], s_all[:, ns:]
    m = s_all.shape[0]
    idx = lax.broadcasted_iota(jnp.int32, (m, 1), 0) % chunks_per_seq
    shift = 1
    while shift < chunks_per_seq:
        live = idx >= shift
        y_re = jnp.where(live, pltpu.roll(x_re, shift, 0), 0.0)
        y_im = jnp.where(live, pltpu.roll(x_im, shift, 0), 0.0)
        x_re, x_im = x_re + p_re * y_re - p_im * y_im, x_im + p_re * y_im + p_im * y_re
        p_re, p_im = p_re * p_re - p_im * p_im, 2.0 * p_re * p_im
        shift *= 2
    first = idx >= 1
    xp = jnp.concatenate([jnp.where(first, pltpu.roll(x_re, 1, 0), 0.0),
                          jnp.where(first, pltpu.roll(x_im, 1, 0), 0.0)], axis=1).astype(BF16)

    cw = toe.shape[1]
    for j in range(cw // tile):
        hi = (j + 1) * tile
        y = jnp.dot(u_ref[0, :, :hi], toe[:hi, j * tile:hi], preferred_element_type=F32)
        y = y + lax.dot_general(xp, wout[j * tile:hi, :], (((1,), (1,)), ((), ())),
                                preferred_element_type=F32)
        y_ref[0, :, j * tile:hi] = y.astype(BF16)


def _ssm_call(ldt, a_re, a_im, b_re, b_im, c_re, c_im, d_skip, u_flat, *, chunks_per_seq, tile):
    n_slab, m, cw = u_flat.shape
    ns = SLAB_STATE
    vec = pl.BlockSpec((1, 1, ns), lambda q: (q, 0, 0))
    mat = pl.BlockSpec((1, LANES, ns), lambda q: (q, 0, 0))
    return pl.pallas_call(
        functools.partial(_ssm_kernel, chunks_per_seq=chunks_per_seq, tile=tile),
        grid=(n_slab,),
        in_specs=[vec, vec, vec, mat, mat, mat, mat, pl.BlockSpec((1, 1, LANES), lambda q: (q, 0, 0)),
                  pl.BlockSpec((1, m, cw), lambda q: (q, 0, 0))],
        out_specs=pl.BlockSpec((1, m, cw), lambda q: (q, 0, 0)),
        out_shape=jax.ShapeDtypeStruct((n_slab, m, cw), BF16),
        scratch_shapes=[pltpu.VMEM((cw, cw), BF16),
                        pltpu.VMEM((cw, 2 * ns), BF16),
                        pltpu.VMEM((cw, 2 * ns), BF16)],
        compiler_params=pltpu.CompilerParams(
            dimension_semantics=("arbitrary",), vmem_limit_bytes=VMEM_LIMIT_BYTES),
        name="ssm",
    )(ldt, a_re, a_im, b_re, b_im, c_re, c_im, d_skip, u_flat)


def _causal_dwconv(gbuf, cw_ref, cb_ref, cbuf, rows):
    base = CONV_HALO - (CONV_KERNEL - 1)
    half = CONV_ROW_BLOCK // 2
    for j in range(gbuf.shape[0]):
        ln = slice(j * LANES, (j + 1) * LANES)
        for i in range(rows // CONV_ROW_BLOCK):
            r0 = i * CONV_ROW_BLOCK
            for parity in range(2):
                acc = jnp.zeros((half, LANES), F32) + cb_ref[:, ln]
                for k in range(CONV_KERNEL):
                    taps = gbuf[j, pl.ds(r0 + parity + base + k, half, stride=2), :]
                    acc = acc + cw_ref[k:k + 1, ln] * taps
                cbuf[j, pl.ds(r0 + parity, half, stride=2), :] = acc


def _back_kernel(x_ref, yzp_ref, szh_ref, ys_ref, wglu_ref, wo0_ref, n1_ref, wi1_ref, cw_ref, cb_ref,
                 lg_ref, lb_ref, wo1_ref, fn_ref, o_ref, gbuf, cbuf, ybuf):
    t = pl.program_id(1)
    rows = x_ref.shape[1]
    n_slab = ys_ref.shape[0]
    c_slab = gbuf.shape[0]
    ch = c_slab * LANES
    chunks = rows // CHUNK

    for q in range(n_slab):
        for st in range(CHUNK):
            ybuf[q, pl.ds(st, chunks, stride=CHUNK), :] = (
                ys_ref[q, :, st * LANES:(st + 1) * LANES].astype(F32))

    @pl.when(t == 0)
    def _():
        gbuf[:, 0:CONV_HALO, :] = jnp.zeros((c_slab, CONV_HALO, LANES), F32)

    sub = rows // ROW_SUBTILES
    x1s, szs = [], []
    for hh in range(ROW_SUBTILES):
        rs = slice(hh * sub, (hh + 1) * sub)
        y_ssm = jnp.concatenate([ybuf[q, rs, :] for q in range(n_slab)], axis=1).astype(BF16)
        gv = jnp.dot(y_ssm, wglu_ref[...], preferred_element_type=F32)
        half = gv.shape[1] // 2
        y_s5 = gv[:, :half] * _sigmoid(gv[:, half:])
        yz_ssm = (y_s5 * szh_ref[0, rs, :].astype(F32)).astype(BF16)
        y0 = jnp.concatenate([yzp_ref[0, rs, :], yz_ssm], axis=1)
        x1 = x_ref[0, rs, :] + jnp.dot(y0, wo0_ref[...], preferred_element_type=F32)

        h1 = _rmsnorm(x1, n1_ref[...]).astype(BF16)
        proj = jnp.dot(h1, wi1_ref[...], preferred_element_type=F32)
        glu = proj[:, :ch] * _sigmoid(proj[:, ch:2 * ch])
        for j in range(c_slab):
            gbuf[j, CONV_HALO + hh * sub:CONV_HALO + (hh + 1) * sub, :] = glu[:, j * LANES:(j + 1) * LANES]
        szs.append(_silu(proj[:, 2 * ch:]))
        x1s.append(x1)

    _causal_dwconv(gbuf, cw_ref, cb_ref, cbuf, rows)
    for j in range(c_slab):
        gbuf[j, 0:CONV_HALO, :] = gbuf[j, rows:rows + CONV_HALO, :]

    for hh in range(ROW_SUBTILES):
        rs = slice(hh * sub, (hh + 1) * sub)
        acc = jnp.concatenate([cbuf[j, rs, :] for j in range(c_slab)], axis=1)
        mu = jnp.mean(acc, axis=-1, keepdims=True)
        cc = acc - mu
        var = jnp.mean(cc * cc, axis=-1, keepdims=True)
        cn = cc * lax.rsqrt(var + LN_EPS) * lg_ref[...] + lb_ref[...]
        y1 = (_silu(cn) * szs[hh]).astype(BF16)
        x2 = x1s[hh] + jnp.dot(y1, wo1_ref[...], preferred_element_type=F32)
        o_ref[0, rs, :] = _rmsnorm(x2, fn_ref[...])


def _back_call(x, yzp, szh, ys, wglu, wo0, n1, wi1, cw, cb, lg, lb, wo1, fn, *, rows):
    b, s, d = x.shape
    n_slab = ys.shape[0]
    grid = (b, s // rows)
    const2 = lambda i, j: (0, 0)
    full = lambda a: pl.BlockSpec(a.shape, const2)
    return pl.pallas_call(
        _back_kernel,
        grid=grid,
        in_specs=[
            pl.BlockSpec((1, rows, d), lambda i, j: (i, j, 0)),
            pl.BlockSpec((1, rows, yzp.shape[2]), lambda i, j: (i, j, 0)),
            pl.BlockSpec((1, rows, szh.shape[2]), lambda i, j: (i, j, 0)),
            pl.BlockSpec((n_slab, rows // CHUNK, CHUNK * LANES), lambda i, j, n=s // rows: (0, i * n + j, 0)),
            full(wglu), full(wo0), full(n1), full(wi1), full(cw), full(cb), full(lg), full(lb),
            full(wo1), full(fn),
        ],
        out_specs=pl.BlockSpec((1, rows, d), lambda i, j: (i, j, 0)),
        out_shape=jax.ShapeDtypeStruct((b, s, d), x.dtype),
        scratch_shapes=[pltpu.VMEM((cw.shape[1] // LANES, rows + CONV_HALO, LANES), F32),
                        pltpu.VMEM((cw.shape[1] // LANES, rows, LANES), F32),
                        pltpu.VMEM((n_slab, rows, LANES), F32)],
        compiler_params=pltpu.CompilerParams(
            dimension_semantics=("arbitrary", "arbitrary"), vmem_limit_bytes=VMEM_LIMIT_BYTES),
        name="back",
    )(x, yzp, szh, ys, wglu, wo0, n1, wi1, cw, cb, lg, lb, wo1, fn)


def _slab_lanes(v, n_slab):
    return v.reshape(n_slab, 1, -1)


def _slab_rows(m, n_slab):
    g, h, p = m.shape
    return jnp.tile(m.reshape(n_slab, (g // n_slab) * h, p), (1, 1, g // n_slab))


def _forward(x, even_norm, even_w_in, pool_w, pool_scale, ssm_log_dt, ssm_a_re, ssm_a_im, ssm_b_re,
             ssm_b_im, ssm_c_re, ssm_c_im, ssm_d, ssm_w_glu, even_w_out, odd_norm, odd_w_in, conv_w,
             conv_b, conv_ln_g, conv_ln_b, odd_w_out, final_norm, *, front_rows, back_rows):
    b, s, d = x.shape
    groups = ssm_log_dt.shape[1]
    n_slab = groups // GROUPS_PER_SLAB

    yzp, szh, u = _front_call(
        x, even_norm[0][None, :], even_w_in[0].astype(BF16), pool_w[0].astype(BF16),
        pool_scale[0][None, :], rows=front_rows)

    ys = _ssm_call(
        _slab_lanes(jnp.repeat(ssm_log_dt[0], SSM_STATE), n_slab),
        _slab_lanes(ssm_a_re[0], n_slab), _slab_lanes(ssm_a_im[0], n_slab),
        _slab_rows(jnp.swapaxes(ssm_b_re[0], 1, 2), n_slab),
        _slab_rows(jnp.swapaxes(ssm_b_im[0], 1, 2), n_slab),
        _slab_rows(ssm_c_re[0], n_slab), _slab_rows(ssm_c_im[0], n_slab),
        ssm_d[0].reshape(n_slab, 1, LANES), u, chunks_per_seq=s // CHUNK, tile=2 * LANES)

    return _back_call(
        x, yzp, szh, ys, ssm_w_glu[0].astype(BF16), even_w_out[0].astype(BF16), odd_norm[0][None, :],
        odd_w_in[0].astype(BF16), conv_w[0], conv_b[0][None, :], conv_ln_g[0][None, :],
        conv_ln_b[0][None, :], odd_w_out[0].astype(BF16), final_norm[None, :], rows=back_rows)


def kernel(x, even_norm, even_w_in, pool_w, pool_scale, ssm_log_dt, ssm_a_re, ssm_a_im, ssm_b_re, ssm_b_im, ssm_c_re, ssm_c_im, ssm_d, ssm_w_glu, even_w_out, odd_norm, odd_w_in, conv_w, conv_b, conv_ln_g, conv_ln_b, odd_w_out, final_norm):
    return _forward(x, even_norm, even_w_in, pool_w, pool_scale, ssm_log_dt, ssm_a_re, ssm_a_im,
                    ssm_b_re, ssm_b_im, ssm_c_re, ssm_c_im, ssm_d, ssm_w_glu, even_w_out, odd_norm,
                    odd_w_in, conv_w, conv_b, conv_ln_g, conv_ln_b, odd_w_out, final_norm,
                    front_rows=1024, back_rows=512)
```

```python
import functools

import jax
import jax.numpy as jnp
from jax import lax
from jax.experimental import pallas as pl
from jax.experimental.pallas import tpu as pltpu

F32 = jnp.float32
BF16 = jnp.bfloat16

RMS_EPS = 1e-6
LN_EPS = 1e-5
POOL_WINDOWS = (2, 4, 8, 16)
POOL_HALO = 16
SSM_GROUP_DIM = 16
SSM_STATE = 64
CONV_KERNEL = 31
CONV_HALO = 32
LANES = 128
CONV_ROW_BLOCK = 64
CHUNK = 16
ROW_SUBTILES = 2
GROUPS_PER_SLAB = LANES // SSM_GROUP_DIM
SLAB_STATE = GROUPS_PER_SLAB * SSM_STATE
VMEM_LIMIT_BYTES = 56 * 1024 * 1024


def _sigmoid(v):
    return 1.0 / (1.0 + jnp.exp(-v))


def _silu(v):
    return v * _sigmoid(v)


def _rmsnorm(v, gain):
    ms = jnp.mean(v * v, axis=-1, keepdims=True)
    return v * lax.rsqrt(ms + RMS_EPS) * gain


def _front_kernel(x_ref, g_ref, w_ref, pw_ref, ps_ref, yzp_ref, szh_ref, u_ref, pbuf, ubuf):
    t = pl.program_id(1)
    rows = x_ref.shape[1]
    pool_width = pbuf.shape[1]
    n_slab = u_ref.shape[0]

    @pl.when(t == 0)
    def _():
        pbuf[0:POOL_HALO, :] = jnp.zeros((POOL_HALO, pool_width), F32)

    mix = w_ref.shape[1] // 2
    sub = rows // ROW_SUBTILES
    for hh in range(ROW_SUBTILES):
        r0 = hh * sub
        rs = slice(r0, r0 + sub)
        h = _rmsnorm(x_ref[0, rs, :], g_ref[...]).astype(BF16)
        proj = jnp.dot(h, w_ref[...], preferred_element_type=F32)
        sz = _silu(proj[:, mix:])

        for q in range(n_slab):
            lo = pool_width + q * LANES
            ubuf[q, rs, :] = proj[:, lo:lo + LANES]
            for st in range(CHUNK):
                u_ref[q, r0 // CHUNK:(r0 + sub) // CHUNK, st * LANES:(st + 1) * LANES] = (
                    ubuf[q, pl.ds(r0 + st, sub // CHUNK, stride=CHUNK), :].astype(BF16))
        szh_ref[0, rs, :] = sz[:, pool_width:].astype(BF16)

        pbuf[POOL_HALO + r0:POOL_HALO + r0 + sub, :] = proj[:, :pool_width]
        pos = (t * rows + r0 + lax.broadcasted_iota(jnp.int32, (sub + POOL_HALO, 1), 0)
               - (POOL_HALO - 1))
        for gi, win in enumerate(POOL_WINDOWS):
            sl = slice(gi * LANES, (gi + 1) * LANES)
            ext = pbuf[r0:r0 + sub + POOL_HALO, sl]
            acc = ext
            shift = 1
            while shift < win:
                acc = acc + pltpu.roll(acc, shift, 0)
                shift *= 2
            inv = 1.0 / jnp.clip(pos, 1, win).astype(F32)
            pooled = (acc * inv - ext)[POOL_HALO:, :]
            y = jnp.dot(pooled.astype(BF16), pw_ref[gi], preferred_element_type=F32) * ps_ref[:, sl]
            yzp_ref[0, rs, sl] = (y * sz[:, sl]).astype(BF16)
    pbuf[0:POOL_HALO, :] = pbuf[rows:rows + POOL_HALO, :]


def _front_call(x, gain, w_in, pool_w, pool_scale, *, rows):
    b, s, d = x.shape
    pool_width = pool_scale.shape[1]
    mix = w_in.shape[1] // 2
    n_slab = (mix - pool_width) // LANES
    grid = (b, s // rows)
    const2 = lambda i, j: (0, 0)
    return pl.pallas_call(
        _front_kernel,
        grid=grid,
        in_specs=[
            pl.BlockSpec((1, rows, d), lambda i, j: (i, j, 0)),
            pl.BlockSpec((1, d), const2),
            pl.BlockSpec(w_in.shape, const2),
            pl.BlockSpec(pool_w.shape, lambda i, j: (0, 0, 0)),
            pl.BlockSpec((1, pool_width), const2),
        ],
        out_specs=[
            pl.BlockSpec((1, rows, pool_width), lambda i, j: (i, j, 0)),
            pl.BlockSpec((1, rows, mix - pool_width), lambda i, j: (i, j, 0)),
            pl.BlockSpec((n_slab, rows // CHUNK, CHUNK * LANES), lambda i, j, n=s // rows: (0, i * n + j, 0)),
        ],
        out_shape=[
            jax.ShapeDtypeStruct((b, s, pool_width), BF16),
            jax.ShapeDtypeStruct((b, s, mix - pool_width), BF16),
            jax.ShapeDtypeStruct((n_slab, b * s // CHUNK, CHUNK * LANES), BF16),
        ],
        scratch_shapes=[pltpu.VMEM((rows + POOL_HALO, pool_width), F32),
                        pltpu.VMEM((n_slab, rows, LANES), F32)],
        compiler_params=pltpu.CompilerParams(
            dimension_semantics=("arbitrary", "arbitrary"), vmem_limit_bytes=VMEM_LIMIT_BYTES),
        name="front",
    )(x, gain, w_in, pool_w, pool_scale)


def _ssm_tables(vec_ref, mat_ref, d_ref, toe, wst, wout, tile):
    ns = SLAB_STATE
    dt = jnp.exp(vec_ref[0, 0:1, :])
    a_re, a_im = vec_ref[0, 1:2, :], vec_ref[0, 2:3, :]
    lam, ang = a_re * dt, a_im * dt
    n_pow = CHUNK + 1
    n_rows = -(-n_pow // 8) * 8
    step = lax.broadcasted_iota(jnp.int32, (n_rows, ns), 0).astype(F32)
    mag = jnp.exp(step * lam)
    pw_re = mag * jnp.cos(step * ang)
    pw_im = mag * jnp.sin(step * ang)

    den = a_re * a_re + a_im * a_im
    nr, ni = pw_re[1:2] - 1.0, pw_im[1:2]
    k_re = (nr * a_re + ni * a_im) / den
    k_im = (ni * a_re - nr * a_im) / den

    row_g = lax.broadcasted_iota(jnp.int32, (LANES, ns), 0) // SSM_GROUP_DIM
    lane_g = lax.broadcasted_iota(jnp.int32, (LANES, ns), 1) // SSM_STATE
    same = row_g == lane_g
    b_re, b_im = mat_ref[0, 0], mat_ref[1, 0]
    bb_re = jnp.where(same, k_re * b_re - k_im * b_im, 0.0)
    bb_im = jnp.where(same, k_re * b_im + k_im * b_re, 0.0)
    c_re = jnp.where(same, mat_ref[2, 0], 0.0)
    c_im = jnp.where(same, mat_ref[3, 0], 0.0)
    c_bf = jnp.concatenate([c_re, -c_im], axis=1).astype(BF16)

    eye = (lax.broadcasted_iota(jnp.int32, (LANES, LANES), 0)
           == lax.broadcasted_iota(jnp.int32, (LANES, LANES), 1))
    zero_blk = jnp.zeros((LANES, LANES), BF16)
    steps_per_tile = tile // LANES
    for n in range(CHUNK):
        p_re, p_im = pw_re[n:n + 1], pw_im[n:n + 1]
        e_cat = jnp.concatenate([p_re * bb_re - p_im * bb_im, p_re * bb_im + p_im * bb_re], axis=1)
        s = CHUNK - 1 - n
        e_bf = e_cat.astype(BF16)
        wst[s * LANES:(s + 1) * LANES, :] = e_bf
        k_blk = lax.dot_general(e_bf, c_bf, (((1,), (1,)), ((), ())), preferred_element_type=F32)
        if n == 0:
            k_blk = k_blk + jnp.where(eye, d_ref[0], 0.0)
        k_bf = k_blk.astype(BF16)
        for s0 in range(CHUNK - n):
            t0 = s0 + n
            toe[s0 * LANES:(s0 + 1) * LANES, t0 * LANES:(t0 + 1) * LANES] = k_bf
    for s0 in range(CHUNK):
        for t0 in range(s0 // steps_per_tile * steps_per_tile, s0):
            toe[s0 * LANES:(s0 + 1) * LANES, t0 * LANES:(t0 + 1) * LANES] = zero_blk
    for t0 in range(CHUNK):
        p_re, p_im = pw_re[t0 + 1:t0 + 2], pw_im[t0 + 1:t0 + 2]
        f_cat = jnp.concatenate([c_re * p_re - c_im * p_im, -(c_re * p_im + c_im * p_re)], axis=1)
        wout[t0 * LANES:(t0 + 1) * LANES, :] = f_cat.astype(BF16)
    return pw_re[CHUNK:CHUNK + 1], pw_im[CHUNK:CHUNK + 1]


def _ssm_kernel(vec_ref, mat_ref, d_ref, u_ref, y_ref, toe, wst, wout, *, chunks_per_seq, tile):
    ns = SLAB_STATE
    p_re, p_im = _ssm_tables(vec_ref, mat_ref, d_ref, toe, wst, wout, tile)

    s_all = jnp.dot(u_ref[0], wst[...], preferred_element_type=F32)
    x_re, x_im = s_all[:, :ns], s_all[:, ns:]
    m = s_all.shape[0]
    idx = lax.broadcasted_iota(jnp.int32, (m, 1), 0) % chunks_per_seq
    shift = 1
    while shift < chunks_per_seq:
        live = idx >= shift
        y_re = jnp.where(live, pltpu.roll(x_re, shift, 0), 0.0)
        y_im = jnp.where(live, pltpu.roll(x_im, shift, 0), 0.0)
        x_re, x_im = x_re + p_re * y_re - p_im * y_im, x_im + p_re * y_im + p_im * y_re
        p_re, p_im = p_re * p_re - p_im * p_im, 2.0 * p_re * p_im
        shift *= 2
    first = idx >= 1
    xp = jnp.concatenate([jnp.where(first, pltpu.roll(x_re, 1, 0), 0.0),
                          jnp.where(first, pltpu.roll(x_im, 1, 0), 0.0)], axis=1).astype(BF16)

    cw = toe.shape[1]
    for j in range(cw // tile):
        hi = (j + 1) * tile
        y = jnp.dot(u_ref[0, :, :hi], toe[:hi, j * tile:hi], preferred_element_type=F32)
        y = y + lax.dot_general(xp, wout[j * tile:hi, :], (((1,), (1,)), ((), ())),
                                preferred_element_type=F32)
        y_ref[0, :, j * tile:hi] = y.astype(BF16)


def _ssm_call(vecs, mats, d_skip, u_flat, *, chunks_per_seq, tile):
    n_slab, m, cw = u_flat.shape
    ns = SLAB_STATE
    return pl.pallas_call(
        functools.partial(_ssm_kernel, chunks_per_seq=chunks_per_seq, tile=tile),
        grid=(n_slab,),
        in_specs=[pl.BlockSpec((1,) + vecs.shape[1:], lambda q: (q, 0, 0)),
                  pl.BlockSpec((mats.shape[0], 1, LANES, ns), lambda q: (0, q, 0, 0)),
                  pl.BlockSpec((1, 1, LANES), lambda q: (q, 0, 0)),
                  pl.BlockSpec((1, m, cw), lambda q: (q, 0, 0))],
        out_specs=pl.BlockSpec((1, m, cw), lambda q: (q, 0, 0)),
        out_shape=jax.ShapeDtypeStruct((n_slab, m, cw), BF16),
        scratch_shapes=[pltpu.VMEM((cw, cw), BF16),
                        pltpu.VMEM((cw, 2 * ns), BF16),
                        pltpu.VMEM((cw, 2 * ns), BF16)],
        compiler_params=pltpu.CompilerParams(
            dimension_semantics=("arbitrary",), vmem_limit_bytes=VMEM_LIMIT_BYTES),
        name="ssm",
    )(vecs, mats, d_skip, u_flat)


def _causal_dwconv(gbuf, cw_ref, cb_ref, cbuf, rows):
    base = CONV_HALO - (CONV_KERNEL - 1)
    half = CONV_ROW_BLOCK // 2
    for j in range(gbuf.shape[0]):
        ln = slice(j * LANES, (j + 1) * LANES)
        for i in range(rows // CONV_ROW_BLOCK):
            r0 = i * CONV_ROW_BLOCK
            for parity in range(2):
                acc = jnp.zeros((half, LANES), F32) + cb_ref[:, ln]
                for k in range(CONV_KERNEL):
                    taps = gbuf[j, pl.ds(r0 + parity + base + k, half, stride=2), :]
                    acc = acc + cw_ref[k:k + 1, ln] * taps
                cbuf[j, pl.ds(r0 + parity, half, stride=2), :] = acc


def _back_kernel(x_ref, yzp_ref, szh_ref, ys_ref, wglu_ref, wo0_ref, n1_ref, wi1_ref, cw_ref, cb_ref,
                 lg_ref, lb_ref, wo1_ref, fn_ref, o_ref, gbuf, cbuf, ybuf):
    t = pl.program_id(1)
    rows = x_ref.shape[1]
    n_slab = ys_ref.shape[0]
    c_slab = gbuf.shape[0]
    ch = c_slab * LANES
    chunks = rows // CHUNK

    for q in range(n_slab):
        for st in range(CHUNK):
            ybuf[q, pl.ds(st, chunks, stride=CHUNK), :] = (
                ys_ref[q, :, st * LANES:(st + 1) * LANES].astype(F32))

    @pl.when(t == 0)
    def _():
        gbuf[:, 0:CONV_HALO, :] = jnp.zeros((c_slab, CONV_HALO, LANES), F32)

    sub = rows // ROW_SUBTILES
    x1s, szs = [], []
    for hh in range(ROW_SUBTILES):
        rs = slice(hh * sub, (hh + 1) * sub)
        y_ssm = jnp.concatenate([ybuf[q, rs, :] for q in range(n_slab)], axis=1).astype(BF16)
        gv = jnp.dot(y_ssm, wglu_ref[...], preferred_element_type=F32)
        half = gv.shape[1] // 2
        y_s5 = gv[:, :half] * _sigmoid(gv[:, half:])
        yz_ssm = (y_s5 * szh_ref[0, rs, :].astype(F32)).astype(BF16)
        y0 = jnp.concatenate([yzp_ref[0, rs, :], yz_ssm], axis=1)
        x1 = x_ref[0, rs, :] + jnp.dot(y0, wo0_ref[...], preferred_element_type=F32)

        h1 = _rmsnorm(x1, n1_ref[...]).astype(BF16)
        proj = jnp.dot(h1, wi1_ref[...], preferred_element_type=F32)
        glu = proj[:, :ch] * _sigmoid(proj[:, ch:2 * ch])
        for j in range(c_slab):
            gbuf[j, CONV_HALO + hh * sub:CONV_HALO + (hh + 1) * sub, :] = glu[:, j * LANES:(j + 1) * LANES]
        szs.append(_silu(proj[:, 2 * ch:]))
        x1s.append(x1)

    _causal_dwconv(gbuf, cw_ref, cb_ref, cbuf, rows)
    for j in range(c_slab):
        gbuf[j, 0:CONV_HALO, :] = gbuf[j, rows:rows + CONV_HALO, :]

    for hh in range(ROW_SUBTILES):
        rs = slice(hh * sub, (hh + 1) * sub)
        acc = jnp.concatenate([cbuf[j, rs, :] for j in range(c_slab)], axis=1)
        mu = jnp.mean(acc, axis=-1, keepdims=True)
        cc = acc - mu
        var = jnp.mean(cc * cc, axis=-1, keepdims=True)
        cn = cc * lax.rsqrt(var + LN_EPS) * lg_ref[...] + lb_ref[...]
        y1 = (_silu(cn) * szs[hh]).astype(BF16)
        x2 = x1s[hh] + jnp.dot(y1, wo1_ref[...], preferred_element_type=F32)
        o_ref[0, rs, :] = _rmsnorm(x2, fn_ref[...])


def _back_call(x, yzp, szh, ys, wglu, wo0, n1, wi1, cw, cb, lg, lb, wo1, fn, *, rows):
    b, s, d = x.shape
    n_slab = ys.shape[0]
    grid = (b, s // rows)
    const2 = lambda i, j: (0, 0)
    full = lambda a: pl.BlockSpec(a.shape, const2)
    return pl.pallas_call(
        _back_kernel,
        grid=grid,
        in_specs=[
            pl.BlockSpec((1, rows, d), lambda i, j: (i, j, 0)),
            pl.BlockSpec((1, rows, yzp.shape[2]), lambda i, j: (i, j, 0)),
            pl.BlockSpec((1, rows, szh.shape[2]), lambda i, j: (i, j, 0)),
            pl.BlockSpec((n_slab, rows // CHUNK, CHUNK * LANES), lambda i, j, n=s // rows: (0, i * n + j, 0)),
            full(wglu), full(wo0), full(n1), full(wi1), full(cw), full(cb), full(lg), full(lb),
            full(wo1), full(fn),
        ],
        out_specs=pl.BlockSpec((1, rows, d), lambda i, j: (i, j, 0)),
        out_shape=jax.ShapeDtypeStruct((b, s, d), x.dtype),
        scratch_shapes=[pltpu.VMEM((cw.shape[1] // LANES, rows + CONV_HALO, LANES), F32),
                        pltpu.VMEM((cw.shape[1] // LANES, rows, LANES), F32),
                        pltpu.VMEM((n_slab, rows, LANES), F32)],
        compiler_params=pltpu.CompilerParams(
            dimension_semantics=("arbitrary", "arbitrary"), vmem_limit_bytes=VMEM_LIMIT_BYTES),
        name="back",
    )(x, yzp, szh, ys, wglu, wo0, n1, wi1, cw, cb, lg, lb, wo1, fn)


def _ssm_param_layout(log_dt, a_re, a_im, b_re, b_im, c_re, c_im, n_slab):
    g, p = a_re.shape
    h = c_re.shape[1]
    per = g // n_slab
    vecs = jnp.stack([jnp.repeat(log_dt, p), a_re.reshape(-1), a_im.reshape(-1)]).reshape(3, n_slab, per * p)
    vecs = jnp.pad(jnp.swapaxes(vecs, 0, 1), ((0, 0), (0, 5), (0, 0)))
    mats = jnp.stack([jnp.swapaxes(b_re, 1, 2), jnp.swapaxes(b_im, 1, 2), c_re, c_im])
    mats = jnp.tile(mats.reshape(4, n_slab, per * h, p), (1, 1, 1, per))
    return vecs, mats


def _forward(x, even_norm, even_w_in, pool_w, pool_scale, ssm_log_dt, ssm_a_re, ssm_a_im, ssm_b_re,
             ssm_b_im, ssm_c_re, ssm_c_im, ssm_d, ssm_w_glu, even_w_out, odd_norm, odd_w_in, conv_w,
             conv_b, conv_ln_g, conv_ln_b, odd_w_out, final_norm, *, front_rows, back_rows):
    b, s, d = x.shape
    groups = ssm_log_dt.shape[1]
    n_slab = groups // GROUPS_PER_SLAB

    yzp, szh, u = _front_call(
        x, even_norm[0][None, :], even_w_in[0].astype(BF16), pool_w[0].astype(BF16),
        pool_scale[0][None, :], rows=front_rows)

    vecs, mats = _ssm_param_layout(ssm_log_dt[0], ssm_a_re[0], ssm_a_im[0], ssm_b_re[0], ssm_b_im[0],
                                   ssm_c_re[0], ssm_c_im[0], n_slab)
    ys = _ssm_call(vecs, mats, ssm_d[0].reshape(n_slab, 1, LANES), u,
                   chunks_per_seq=s // CHUNK, tile=2 * LANES)

    return _back_call(
        x, yzp, szh, ys, ssm_w_glu[0].astype(BF16), even_w_out[0].astype(BF16), odd_norm[0][None, :],
        odd_w_in[0].astype(BF16), conv_w[0], conv_b[0][None, :], conv_ln_g[0][None, :],
        conv_ln_b[0][None, :], odd_w_out[0].astype(BF16), final_norm[None, :], rows=back_rows)


def kernel(x, even_norm, even_w_in, pool_w, pool_scale, ssm_log_dt, ssm_a_re, ssm_a_im, ssm_b_re, ssm_b_im, ssm_c_re, ssm_c_im, ssm_d, ssm_w_glu, even_w_out, odd_norm, odd_w_in, conv_w, conv_b, conv_ln_g, conv_ln_b, odd_w_out, final_norm):
    return _forward(x, even_norm, even_w_in, pool_w, pool_scale, ssm_log_dt, ssm_a_re, ssm_a_im,
                    ssm_b_re, ssm_b_im, ssm_c_re, ssm_c_im, ssm_d, ssm_w_glu, even_w_out, odd_norm,
                    odd_w_in, conv_w, conv_b, conv_ln_g, conv_ln_b, odd_w_out, final_norm,
                    front_rows=2048, back_rows=512)
```

```python
import functools

import jax
import jax.numpy as jnp
from jax import lax
from jax.experimental import pallas as pl
from jax.experimental.pallas import tpu as pltpu

F32 = jnp.float32
BF16 = jnp.bfloat16

RMS_EPS = 1e-6
LN_EPS = 1e-5
POOL_WINDOWS = (2, 4, 8, 16)
POOL_HALO = 16
SSM_GROUP_DIM = 16
SSM_STATE = 64
CONV_KERNEL = 31
CONV_HALO = 32
LANES = 128
CONV_ROW_BLOCK = 64
CHUNK = 16
ROW_SUBTILES = 2
GROUPS_PER_SLAB = LANES // SSM_GROUP_DIM
SLAB_STATE = GROUPS_PER_SLAB * SSM_STATE
VMEM_LIMIT_BYTES = 56 * 1024 * 1024


def _sigmoid(v):
    return 1.0 / (1.0 + jnp.exp(-v))


def _silu(v):
    return v * _sigmoid(v)


def _rmsnorm(v, gain):
    ms = jnp.mean(v * v, axis=-1, keepdims=True)
    return v * lax.rsqrt(ms + RMS_EPS) * gain


def _front_kernel(x_ref, g_ref, w32_ref, pw_ref, ps_ref, yzp_ref, szh_ref, u_ref, pbuf, ubuf, w_ref):
    t = pl.program_id(1)

    @pl.when((t == 0) & (pl.program_id(0) == 0))
    def _():
        w_ref[...] = w32_ref[...].astype(BF16)

    rows = x_ref.shape[1]
    pool_width = pbuf.shape[1]
    n_slab = u_ref.shape[0]

    @pl.when(t == 0)
    def _():
        pbuf[0:POOL_HALO, :] = jnp.zeros((POOL_HALO, pool_width), F32)

    mix = w_ref.shape[1] // 2
    sub = rows // ROW_SUBTILES
    for hh in range(ROW_SUBTILES):
        r0 = hh * sub
        rs = slice(r0, r0 + sub)
        h = _rmsnorm(x_ref[0, rs, :], g_ref[...]).astype(BF16)
        proj = jnp.dot(h, w_ref[...], preferred_element_type=F32)
        sz = _silu(proj[:, mix:])

        for q in range(n_slab):
            lo = pool_width + q * LANES
            ubuf[q, rs, :] = proj[:, lo:lo + LANES]
            for st in range(CHUNK):
                u_ref[q, r0 // CHUNK:(r0 + sub) // CHUNK, st * LANES:(st + 1) * LANES] = (
                    ubuf[q, pl.ds(r0 + st, sub // CHUNK, stride=CHUNK), :].astype(BF16))
        szh_ref[0, rs, :] = sz[:, pool_width:].astype(BF16)

        pbuf[POOL_HALO + r0:POOL_HALO + r0 + sub, :] = proj[:, :pool_width]
        pos = (t * rows + r0 + lax.broadcasted_iota(jnp.int32, (sub + POOL_HALO, 1), 0)
               - (POOL_HALO - 1))
        for gi, win in enumerate(POOL_WINDOWS):
            sl = slice(gi * LANES, (gi + 1) * LANES)
            ext = pbuf[r0:r0 + sub + POOL_HALO, sl]
            acc = ext
            shift = 1
            while shift < win:
                acc = acc + pltpu.roll(acc, shift, 0)
                shift *= 2
            inv = 1.0 / jnp.clip(pos, 1, win).astype(F32)
            pooled = (acc * inv - ext)[POOL_HALO:, :]
            y = jnp.dot(pooled.astype(BF16), pw_ref[gi], preferred_element_type=F32) * ps_ref[:, sl]
            yzp_ref[0, rs, sl] = (y * sz[:, sl]).astype(BF16)
    pbuf[0:POOL_HALO, :] = pbuf[rows:rows + POOL_HALO, :]


def _front_call(x, gain, w_in, pool_w, pool_scale, *, rows):
    b, s, d = x.shape
    pool_width = pool_scale.shape[1]
    mix = w_in.shape[1] // 2
    n_slab = (mix - pool_width) // LANES
    grid = (b, s // rows)
    const2 = lambda i, j: (0, 0)
    return pl.pallas_call(
        _front_kernel,
        grid=grid,
        in_specs=[
            pl.BlockSpec((1, rows, d), lambda i, j: (i, j, 0)),
            pl.BlockSpec((1, d), const2),
            pl.BlockSpec(w_in.shape, const2, pipeline_mode=pl.Buffered(1)),
            pl.BlockSpec(pool_w.shape, lambda i, j: (0, 0, 0)),
            pl.BlockSpec((1, pool_width), const2),
        ],
        out_specs=[
            pl.BlockSpec((1, rows, pool_width), lambda i, j: (i, j, 0)),
            pl.BlockSpec((1, rows, mix - pool_width), lambda i, j: (i, j, 0)),
            pl.BlockSpec((n_slab, rows // CHUNK, CHUNK * LANES), lambda i, j, n=s // rows: (0, i * n + j, 0)),
        ],
        out_shape=[
            jax.ShapeDtypeStruct((b, s, pool_width), BF16),
            jax.ShapeDtypeStruct((b, s, mix - pool_width), BF16),
            jax.ShapeDtypeStruct((n_slab, b * s // CHUNK, CHUNK * LANES), BF16),
        ],
        scratch_shapes=[pltpu.VMEM((rows + POOL_HALO, pool_width), F32),
                        pltpu.VMEM((n_slab, rows, LANES), F32),
                        pltpu.VMEM(w_in.shape, BF16)],
        compiler_params=pltpu.CompilerParams(
            dimension_semantics=("arbitrary", "arbitrary"), vmem_limit_bytes=VMEM_LIMIT_BYTES),
        name="front",
    )(x, gain, w_in, pool_w, pool_scale)


def _ssm_tables(vec_ref, mat_ref, d_ref, toe, wst, wout, tile):
    ns = SLAB_STATE
    dt = jnp.exp(vec_ref[0, 0:1, :])
    a_re, a_im = vec_ref[0, 1:2, :], vec_ref[0, 2:3, :]
    lam, ang = a_re * dt, a_im * dt
    n_pow = CHUNK + 1
    n_rows = -(-n_pow // 8) * 8
    step = lax.broadcasted_iota(jnp.int32, (n_rows, ns), 0).astype(F32)
    mag = jnp.exp(step * lam)
    pw_re = mag * jnp.cos(step * ang)
    pw_im = mag * jnp.sin(step * ang)

    den = a_re * a_re + a_im * a_im
    nr, ni = pw_re[1:2] - 1.0, pw_im[1:2]
    k_re = (nr * a_re + ni * a_im) / den
    k_im = (ni * a_re - nr * a_im) / den

    row_g = lax.broadcasted_iota(jnp.int32, (LANES, ns), 0) // SSM_GROUP_DIM
    lane_g = lax.broadcasted_iota(jnp.int32, (LANES, ns), 1) // SSM_STATE
    same = row_g == lane_g
    b_re, b_im = mat_ref[0, 0], mat_ref[1, 0]
    bb_re = jnp.where(same, k_re * b_re - k_im * b_im, 0.0)
    bb_im = jnp.where(same, k_re * b_im + k_im * b_re, 0.0)
    c_re = jnp.where(same, mat_ref[2, 0], 0.0)
    c_im = jnp.where(same, mat_ref[3, 0], 0.0)
    c_bf = jnp.concatenate([c_re, -c_im], axis=1).astype(BF16)

    eye = (lax.broadcasted_iota(jnp.int32, (LANES, LANES), 0)
           == lax.broadcasted_iota(jnp.int32, (LANES, LANES), 1))
    zero_blk = jnp.zeros((LANES, LANES), BF16)
    steps_per_tile = tile // LANES
    for n in range(CHUNK):
        p_re, p_im = pw_re[n:n + 1], pw_im[n:n + 1]
        e_cat = jnp.concatenate([p_re * bb_re - p_im * bb_im, p_re * bb_im + p_im * bb_re], axis=1)
        s = CHUNK - 1 - n
        e_bf = e_cat.astype(BF16)
        wst[s * LANES:(s + 1) * LANES, :] = e_bf
        k_blk = lax.dot_general(e_bf, c_bf, (((1,), (1,)), ((), ())), preferred_element_type=F32)
        if n == 0:
            k_blk = k_blk + jnp.where(eye, d_ref[0], 0.0)
        k_bf = k_blk.astype(BF16)
        for s0 in range(CHUNK - n):
            t0 = s0 + n
            toe[s0 * LANES:(s0 + 1) * LANES, t0 * LANES:(t0 + 1) * LANES] = k_bf
    for s0 in range(CHUNK):
        for t0 in range(s0 // steps_per_tile * steps_per_tile, s0):
            toe[s0 * LANES:(s0 + 1) * LANES, t0 * LANES:(t0 + 1) * LANES] = zero_blk
    for t0 in range(CHUNK):
        p_re, p_im = pw_re[t0 + 1:t0 + 2], pw_im[t0 + 1:t0 + 2]
        f_cat = jnp.concatenate([c_re * p_re - c_im * p_im, -(c_re * p_im + c_im * p_re)], axis=1)
        wout[t0 * LANES:(t0 + 1) * LANES, :] = f_cat.astype(BF16)
    return pw_re[CHUNK:CHUNK + 1], pw_im[CHUNK:CHUNK + 1]


def _ssm_kernel(vec_ref, mat_ref, d_ref, u_ref, y_ref, toe, wst, wout, *, chunks_per_seq, tile):
    ns = SLAB_STATE
    p_re, p_im = _ssm_tables(vec_ref, mat_ref, d_ref, toe, wst, wout, tile)

    s_all = jnp.dot(u_ref[0], wst[...], preferred_element_type=F32)
    x_re, x_im = s_all[:, :ns], s_all[:, ns:]
    m = s_all.shape[0]
    idx = lax.broadcasted_iota(jnp.int32, (m, 1), 0) % chunks_per_seq
    shift = 1
    while shift < chunks_per_seq:
        live = idx >= shift
        y_re = jnp.where(live, pltpu.roll(x_re, shift, 0), 0.0)
        y_im = jnp.where(live, pltpu.roll(x_im, shift, 0), 0.0)
        x_re, x_im = x_re + p_re * y_re - p_im * y_im, x_im + p_re * y_im + p_im * y_re
        p_re, p_im = p_re * p_re - p_im * p_im, 2.0 * p_re * p_im
        shift *= 2
    first = idx >= 1
    xp = jnp.concatenate([jnp.where(first, pltpu.roll(x_re, 1, 0), 0.0),
                          jnp.where(first, pltpu.roll(x_im, 1, 0), 0.0)], axis=1).astype(BF16)

    cw = toe.shape[1]
    for j in range(cw // tile):
        hi = (j + 1) * tile
        y = jnp.dot(u_ref[0, :, :hi], toe[:hi, j * tile:hi], preferred_element_type=F32)
        y = y + lax.dot_general(xp, wout[j * tile:hi, :], (((1,), (1,)), ((), ())),
                                preferred_element_type=F32)
        y_ref[0, :, j * tile:hi] = y.astype(BF16)


def _ssm_call(vecs, mats, d_skip, u_flat, *, chunks_per_seq, tile):
    n_slab, m, cw = u_flat.shape
    ns = SLAB_STATE
    return pl.pallas_call(
        functools.partial(_ssm_kernel, chunks_per_seq=chunks_per_seq, tile=tile),
        grid=(n_slab,),
        in_specs=[pl.BlockSpec((1,) + vecs.shape[1:], lambda q: (q, 0, 0)),
                  pl.BlockSpec((mats.shape[0], 1, LANES, ns), lambda q: (0, q, 0, 0)),
                  pl.BlockSpec((1, 1, LANES), lambda q: (q, 0, 0)),
                  pl.BlockSpec((1, m, cw), lambda q: (q, 0, 0))],
        out_specs=pl.BlockSpec((1, m, cw), lambda q: (q, 0, 0)),
        out_shape=jax.ShapeDtypeStruct((n_slab, m, cw), BF16),
        scratch_shapes=[pltpu.VMEM((cw, cw), BF16),
                        pltpu.VMEM((cw, 2 * ns), BF16),
                        pltpu.VMEM((cw, 2 * ns), BF16)],
        compiler_params=pltpu.CompilerParams(
            dimension_semantics=("arbitrary",), vmem_limit_bytes=VMEM_LIMIT_BYTES),
        name="ssm",
    )(vecs, mats, d_skip, u_flat)


def _causal_dwconv(gbuf, cw_ref, cb_ref, cbuf, rows):
    base = CONV_HALO - (CONV_KERNEL - 1)
    half = CONV_ROW_BLOCK // 2
    for j in range(gbuf.shape[0]):
        ln = slice(j * LANES, (j + 1) * LANES)
        for i in range(rows // CONV_ROW_BLOCK):
            r0 = i * CONV_ROW_BLOCK
            for parity in range(2):
                acc = jnp.zeros((half, LANES), F32) + cb_ref[:, ln]
                for k in range(CONV_KERNEL):
                    taps = gbuf[j, pl.ds(r0 + parity + base + k, half, stride=2), :]
                    acc = acc + cw_ref[k:k + 1, ln] * taps
                cbuf[j, pl.ds(r0 + parity, half, stride=2), :] = acc


def _back_kernel(x_ref, yzp_ref, szh_ref, ys_ref, wglu32_ref, wo032_ref, n1_ref, wi1_ref, cw_ref, cb_ref,
                 lg_ref, lb_ref, wo132_ref, fn_ref, o_ref, gbuf, cbuf, ybuf, wglu_ref, wo0_ref, wo1_ref):
    t = pl.program_id(1)

    @pl.when((t == 0) & (pl.program_id(0) == 0))
    def _():
        wglu_ref[...] = wglu32_ref[...].astype(BF16)
        wo0_ref[...] = wo032_ref[...].astype(BF16)
        wo1_ref[...] = wo132_ref[...].astype(BF16)

    rows = x_ref.shape[1]
    n_slab = ys_ref.shape[0]
    c_slab = gbuf.shape[0]
    ch = c_slab * LANES
    chunks = rows // CHUNK

    for q in range(n_slab):
        for st in range(CHUNK):
            ybuf[q, pl.ds(st, chunks, stride=CHUNK), :] = (
                ys_ref[q, :, st * LANES:(st + 1) * LANES].astype(F32))

    @pl.when(t == 0)
    def _():
        gbuf[:, 0:CONV_HALO, :] = jnp.zeros((c_slab, CONV_HALO, LANES), F32)

    sub = rows // ROW_SUBTILES
    x1s, szs = [], []
    for hh in range(ROW_SUBTILES):
        rs = slice(hh * sub, (hh + 1) * sub)
        y_ssm = jnp.concatenate([ybuf[q, rs, :] for q in range(n_slab)], axis=1).astype(BF16)
        gv = jnp.dot(y_ssm, wglu_ref[...], preferred_element_type=F32)
        half = gv.shape[1] // 2
        y_s5 = gv[:, :half] * _sigmoid(gv[:, half:])
        yz_ssm = (y_s5 * szh_ref[0, rs, :].astype(F32)).astype(BF16)
        y0 = jnp.concatenate([yzp_ref[0, rs, :], yz_ssm], axis=1)
        x1 = x_ref[0, rs, :] + jnp.dot(y0, wo0_ref[...], preferred_element_type=F32)

        h1 = _rmsnorm(x1, n1_ref[...]).astype(BF16)
        proj = jnp.dot(h1, wi1_ref[...], preferred_element_type=F32)
        glu = proj[:, :ch] * _sigmoid(proj[:, ch:2 * ch])
        for j in range(c_slab):
            gbuf[j, CONV_HALO + hh * sub:CONV_HALO + (hh + 1) * sub, :] = glu[:, j * LANES:(j + 1) * LANES]
        szs.append(_silu(proj[:, 2 * ch:]))
        x1s.append(x1)

    _causal_dwconv(gbuf, cw_ref, cb_ref, cbuf, rows)
    for j in range(c_slab):
        gbuf[j, 0:CONV_HALO, :] = gbuf[j, rows:rows + CONV_HALO, :]

    for hh in range(ROW_SUBTILES):
        rs = slice(hh * sub, (hh + 1) * sub)
        acc = jnp.concatenate([cbuf[j, rs, :] for j in range(c_slab)], axis=1)
        mu = jnp.mean(acc, axis=-1, keepdims=True)
        cc = acc - mu
        var = jnp.mean(cc * cc, axis=-1, keepdims=True)
        cn = cc * lax.rsqrt(var + LN_EPS) * lg_ref[...] + lb_ref[...]
        y1 = (_silu(cn) * szs[hh]).astype(BF16)
        x2 = x1s[hh] + jnp.dot(y1, wo1_ref[...], preferred_element_type=F32)
        o_ref[0, rs, :] = _rmsnorm(x2, fn_ref[...])


def _back_call(x, yzp, szh, ys, wglu, wo0, n1, wi1, cw, cb, lg, lb, wo1, fn, *, rows):
    b, s, d = x.shape
    n_slab = ys.shape[0]
    grid = (b, s // rows)
    const2 = lambda i, j: (0, 0)
    full = lambda a: pl.BlockSpec(a.shape, const2)
    once = lambda a: pl.BlockSpec(a.shape, const2, pipeline_mode=pl.Buffered(1))
    return pl.pallas_call(
        _back_kernel,
        grid=grid,
        in_specs=[
            pl.BlockSpec((1, rows, d), lambda i, j: (i, j, 0)),
            pl.BlockSpec((1, rows, yzp.shape[2]), lambda i, j: (i, j, 0)),
            pl.BlockSpec((1, rows, szh.shape[2]), lambda i, j: (i, j, 0)),
            pl.BlockSpec((n_slab, rows // CHUNK, CHUNK * LANES), lambda i, j, n=s // rows: (0, i * n + j, 0)),
            once(wglu), once(wo0), full(n1), once(wi1), full(cw), full(cb), full(lg), full(lb),
            once(wo1), full(fn),
        ],
        out_specs=pl.BlockSpec((1, rows, d), lambda i, j: (i, j, 0)),
        out_shape=jax.ShapeDtypeStruct((b, s, d), x.dtype),
        scratch_shapes=[pltpu.VMEM((cw.shape[1] // LANES, rows + CONV_HALO, LANES), F32),
                        pltpu.VMEM((cw.shape[1] // LANES, rows, LANES), F32),
                        pltpu.VMEM((n_slab, rows, LANES), F32),
                        pltpu.VMEM(wglu.shape, BF16), pltpu.VMEM(wo0.shape, BF16), pltpu.VMEM(wo1.shape, BF16)],
        compiler_params=pltpu.CompilerParams(
            dimension_semantics=("arbitrary", "arbitrary"), vmem_limit_bytes=VMEM_LIMIT_BYTES),
        name="back",
    )(x, yzp, szh, ys, wglu, wo0, n1, wi1, cw, cb, lg, lb, wo1, fn)


def _ssm_param_layout(log_dt, a_re, a_im, b_re, b_im, c_re, c_im, n_slab):
    g, p = a_re.shape
    h = c_re.shape[1]
    per = g // n_slab
    vecs = jnp.stack([jnp.repeat(log_dt, p), a_re.reshape(-1), a_im.reshape(-1)]).reshape(3, n_slab, per * p)
    vecs = jnp.pad(jnp.swapaxes(vecs, 0, 1), ((0, 0), (0, 5), (0, 0)))
    mats = jnp.stack([jnp.swapaxes(b_re, 1, 2), jnp.swapaxes(b_im, 1, 2), c_re, c_im])
    mats = jnp.tile(mats.reshape(4, n_slab, per * h, p), (1, 1, 1, per))
    return vecs, mats


def _forward(x, even_norm, even_w_in, pool_w, pool_scale, ssm_log_dt, ssm_a_re, ssm_a_im, ssm_b_re,
             ssm_b_im, ssm_c_re, ssm_c_im, ssm_d, ssm_w_glu, even_w_out, odd_norm, odd_w_in, conv_w,
             conv_b, conv_ln_g, conv_ln_b, odd_w_out, final_norm, *, front_rows, back_rows):
    b, s, d = x.shape
    groups = ssm_log_dt.shape[1]
    n_slab = groups // GROUPS_PER_SLAB

    yzp, szh, u = _front_call(
        x, even_norm[0][None, :], even_w_in[0], pool_w[0].astype(BF16),
        pool_scale[0][None, :], rows=front_rows)

    vecs, mats = _ssm_param_layout(ssm_log_dt[0], ssm_a_re[0], ssm_a_im[0], ssm_b_re[0], ssm_b_im[0],
                                   ssm_c_re[0], ssm_c_im[0], n_slab)
    ys = _ssm_call(vecs, mats, ssm_d[0].reshape(n_slab, 1, LANES), u,
                   chunks_per_seq=s // CHUNK, tile=2 * LANES)

    return _back_call(
        x, yzp, szh, ys, ssm_w_glu[0], even_w_out[0], odd_norm[0][None, :],
        odd_w_in[0].astype(BF16), conv_w[0], conv_b[0][None, :], conv_ln_g[0][None, :],
        conv_ln_b[0][None, :], odd_w_out[0], final_norm[None, :], rows=back_rows)


def kernel(x, even_norm, even_w_in, pool_w, pool_scale, ssm_log_dt, ssm_a_re, ssm_a_im, ssm_b_re, ssm_b_im, ssm_c_re, ssm_c_im, ssm_d, ssm_w_glu, even_w_out, odd_norm, odd_w_in, conv_w, conv_b, conv_ln_g, conv_ln_b, odd_w_out, final_norm):
    return _forward(x, even_norm, even_w_in, pool_w, pool_scale, ssm_log_dt, ssm_a_re, ssm_a_im,
                    ssm_b_re, ssm_b_im, ssm_c_re, ssm_c_im, ssm_d, ssm_w_glu, even_w_out, odd_norm,
                    odd_w_in, conv_w, conv_b, conv_ln_g, conv_ln_b, odd_w_out, final_norm,
                    front_rows=2048, back_rows=512)
```

```python
import functools

import jax
import jax.numpy as jnp
from jax import lax
from jax.experimental import pallas as pl
from jax.experimental.pallas import tpu as pltpu

F32 = jnp.float32
BF16 = jnp.bfloat16

RMS_EPS = 1e-6
LN_EPS = 1e-5
POOL_WINDOWS = (2, 4, 8, 16)
POOL_HALO = 16
SSM_GROUP_DIM = 16
SSM_STATE = 64
CONV_KERNEL = 31
CONV_HALO = 32
LANES = 128
CONV_ROW_BLOCK = 256
CHUNK = 16
ROW_SUBTILES = 2
GROUPS_PER_SLAB = LANES // SSM_GROUP_DIM
SLAB_STATE = GROUPS_PER_SLAB * SSM_STATE
VMEM_LIMIT_BYTES = 56 * 1024 * 1024


def _sigmoid(v):
    return 1.0 / (1.0 + jnp.exp(-v))


def _silu(v):
    return v * _sigmoid(v)


def _rmsnorm(v, gain):
    ms = jnp.mean(v * v, axis=-1, keepdims=True)
    return v * lax.rsqrt(ms + RMS_EPS) * gain


def _front_kernel(x_ref, g_ref, w32_ref, pw_ref, ps_ref, yzp_ref, szh_ref, u_ref, pbuf, ubuf, w_ref):
    t = pl.program_id(1)

    @pl.when((t == 0) & (pl.program_id(0) == 0))
    def _():
        w_ref[...] = w32_ref[...].astype(BF16)

    rows = x_ref.shape[1]
    pool_width = pbuf.shape[1]
    n_slab = u_ref.shape[0]

    @pl.when(t == 0)
    def _():
        pbuf[0:POOL_HALO, :] = jnp.zeros((POOL_HALO, pool_width), F32)

    mix = w_ref.shape[1] // 2
    sub = rows // ROW_SUBTILES
    for hh in range(ROW_SUBTILES):
        r0 = hh * sub
        rs = slice(r0, r0 + sub)
        h = _rmsnorm(x_ref[0, rs, :], g_ref[...]).astype(BF16)
        proj = jnp.dot(h, w_ref[...], preferred_element_type=F32)
        sz = _silu(proj[:, mix:])

        for q in range(n_slab):
            lo = pool_width + q * LANES
            ubuf[q, rs, :] = proj[:, lo:lo + LANES]
            for st in range(CHUNK):
                u_ref[q, r0 // CHUNK:(r0 + sub) // CHUNK, st * LANES:(st + 1) * LANES] = (
                    ubuf[q, pl.ds(r0 + st, sub // CHUNK, stride=CHUNK), :].astype(BF16))
        szh_ref[0, rs, :] = sz[:, pool_width:].astype(BF16)

        pbuf[POOL_HALO + r0:POOL_HALO + r0 + sub, :] = proj[:, :pool_width]
        pos = (t * rows + r0 + lax.broadcasted_iota(jnp.int32, (sub + POOL_HALO, 1), 0)
               - (POOL_HALO - 1))
        for gi, win in enumerate(POOL_WINDOWS):
            sl = slice(gi * LANES, (gi + 1) * LANES)
            ext = pbuf[r0:r0 + sub + POOL_HALO, sl]
            acc = ext
            shift = 1
            while shift < win:
                acc = acc + pltpu.roll(acc, shift, 0)
                shift *= 2
            inv = 1.0 / jnp.clip(pos, 1, win).astype(F32)
            pooled = (acc * inv - ext)[POOL_HALO:, :]
            y = jnp.dot(pooled.astype(BF16), pw_ref[gi], preferred_element_type=F32) * ps_ref[:, sl]
            yzp_ref[0, rs, sl] = (y * sz[:, sl]).astype(BF16)
    pbuf[0:POOL_HALO, :] = pbuf[rows:rows + POOL_HALO, :]


def _front_call(x, gain, w_in, pool_w, pool_scale, *, rows):
    b, s, d = x.shape
    pool_width = pool_scale.shape[1]
    mix = w_in.shape[1] // 2
    n_slab = (mix - pool_width) // LANES
    grid = (b, s // rows)
    const2 = lambda i, j: (0, 0)
    return pl.pallas_call(
        _front_kernel,
        grid=grid,
        in_specs=[
            pl.BlockSpec((1, rows, d), lambda i, j: (i, j, 0)),
            pl.BlockSpec((1, d), const2),
            pl.BlockSpec(w_in.shape, const2, pipeline_mode=pl.Buffered(1)),
            pl.BlockSpec(pool_w.shape, lambda i, j: (0, 0, 0)),
            pl.BlockSpec((1, pool_width), const2),
        ],
        out_specs=[
            pl.BlockSpec((1, rows, pool_width), lambda i, j: (i, j, 0)),
            pl.BlockSpec((1, rows, mix - pool_width), lambda i, j: (i, j, 0)),
            pl.BlockSpec((n_slab, rows // CHUNK, CHUNK * LANES), lambda i, j, n=s // rows: (0, i * n + j, 0)),
        ],
        out_shape=[
            jax.ShapeDtypeStruct((b, s, pool_width), BF16),
            jax.ShapeDtypeStruct((b, s, mix - pool_width), BF16),
            jax.ShapeDtypeStruct((n_slab, b * s // CHUNK, CHUNK * LANES), BF16),
        ],
        scratch_shapes=[pltpu.VMEM((rows + POOL_HALO, pool_width), F32),
                        pltpu.VMEM((n_slab, rows, LANES), F32),
                        pltpu.VMEM(w_in.shape, BF16)],
        compiler_params=pltpu.CompilerParams(
            dimension_semantics=("arbitrary", "arbitrary"), vmem_limit_bytes=VMEM_LIMIT_BYTES),
        name="front",
    )(x, gain, w_in, pool_w, pool_scale)


def _ssm_tables(vec_ref, mat_ref, d_ref, toe, wst, wout, tile):
    ns = SLAB_STATE
    dt = jnp.exp(vec_ref[0, 0:1, :])
    a_re, a_im = vec_ref[0, 1:2, :], vec_ref[0, 2:3, :]
    lam, ang = a_re * dt, a_im * dt
    n_pow = CHUNK + 1
    n_rows = -(-n_pow // 8) * 8
    step = lax.broadcasted_iota(jnp.int32, (n_rows, ns), 0).astype(F32)
    mag = jnp.exp(step * lam)
    pw_re = mag * jnp.cos(step * ang)
    pw_im = mag * jnp.sin(step * ang)

    den = a_re * a_re + a_im * a_im
    nr, ni = pw_re[1:2] - 1.0, pw_im[1:2]
    k_re = (nr * a_re + ni * a_im) / den
    k_im = (ni * a_re - nr * a_im) / den

    row_g = lax.broadcasted_iota(jnp.int32, (LANES, ns), 0) // SSM_GROUP_DIM
    lane_g = lax.broadcasted_iota(jnp.int32, (LANES, ns), 1) // SSM_STATE
    same = row_g == lane_g
    b_re, b_im = mat_ref[0, 0], mat_ref[1, 0]
    bb_re = jnp.where(same, k_re * b_re - k_im * b_im, 0.0)
    bb_im = jnp.where(same, k_re * b_im + k_im * b_re, 0.0)
    c_re = jnp.where(same, mat_ref[2, 0], 0.0)
    c_im = jnp.where(same, mat_ref[3, 0], 0.0)
    c_bf = jnp.concatenate([c_re, -c_im], axis=1).astype(BF16)

    eye = (lax.broadcasted_iota(jnp.int32, (LANES, LANES), 0)
           == lax.broadcasted_iota(jnp.int32, (LANES, LANES), 1))
    zero_blk = jnp.zeros((LANES, LANES), BF16)
    steps_per_tile = tile // LANES
    for n in range(CHUNK):
        p_re, p_im = pw_re[n:n + 1], pw_im[n:n + 1]
        e_cat = jnp.concatenate([p_re * bb_re - p_im * bb_im, p_re * bb_im + p_im * bb_re], axis=1)
        s = CHUNK - 1 - n
        e_bf = e_cat.astype(BF16)
        wst[s * LANES:(s + 1) * LANES, :] = e_bf
        k_blk = lax.dot_general(e_bf, c_bf, (((1,), (1,)), ((), ())), preferred_element_type=F32)
        if n == 0:
            k_blk = k_blk + jnp.where(eye, d_ref[0], 0.0)
        k_bf = k_blk.astype(BF16)
        for s0 in range(CHUNK - n):
            t0 = s0 + n
            toe[s0 * LANES:(s0 + 1) * LANES, t0 * LANES:(t0 + 1) * LANES] = k_bf
    for s0 in range(CHUNK):
        for t0 in range(s0 // steps_per_tile * steps_per_tile, s0):
            toe[s0 * LANES:(s0 + 1) * LANES, t0 * LANES:(t0 + 1) * LANES] = zero_blk
    for t0 in range(CHUNK):
        p_re, p_im = pw_re[t0 + 1:t0 + 2], pw_im[t0 + 1:t0 + 2]
        f_cat = jnp.concatenate([c_re * p_re - c_im * p_im, -(c_re * p_im + c_im * p_re)], axis=1)
        wout[t0 * LANES:(t0 + 1) * LANES, :] = f_cat.astype(BF16)
    return pw_re[CHUNK:CHUNK + 1], pw_im[CHUNK:CHUNK + 1]


def _ssm_kernel(vec_ref, mat_ref, d_ref, u_ref, y_ref, toe, wst, wout, *, chunks_per_seq, tile):
    ns = SLAB_STATE
    p_re, p_im = _ssm_tables(vec_ref, mat_ref, d_ref, toe, wst, wout, tile)

    s_all = jnp.dot(u_ref[0], wst[...], preferred_element_type=F32)
    x_re, x_im = s_all[:, :ns], s_all[:, ns:]
    m = s_all.shape[0]
    idx = lax.broadcasted_iota(jnp.int32, (m, 1), 0) % chunks_per_seq
    shift = 1
    while shift < chunks_per_seq:
        live = idx >= shift
        y_re = jnp.where(live, pltpu.roll(x_re, shift, 0), 0.0)
        y_im = jnp.where(live, pltpu.roll(x_im, shift, 0), 0.0)
        x_re, x_im = x_re + p_re * y_re - p_im * y_im, x_im + p_re * y_im + p_im * y_re
        p_re, p_im = p_re * p_re - p_im * p_im, 2.0 * p_re * p_im
        shift *= 2
    first = idx >= 1
    xp = jnp.concatenate([jnp.where(first, pltpu.roll(x_re, 1, 0), 0.0),
                          jnp.where(first, pltpu.roll(x_im, 1, 0), 0.0)], axis=1).astype(BF16)

    cw = toe.shape[1]
    for j in range(cw // tile):
        hi = (j + 1) * tile
        y = jnp.dot(u_ref[0, :, :hi], toe[:hi, j * tile:hi], preferred_element_type=F32)
        y = y + lax.dot_general(xp, wout[j * tile:hi, :], (((1,), (1,)), ((), ())),
                                preferred_element_type=F32)
        y_ref[0, :, j * tile:hi] = y.astype(BF16)


def _ssm_call(vecs, mats, d_skip, u_flat, *, chunks_per_seq, tile):
    n_slab, m, cw = u_flat.shape
    ns = SLAB_STATE
    return pl.pallas_call(
        functools.partial(_ssm_kernel, chunks_per_seq=chunks_per_seq, tile=tile),
        grid=(n_slab,),
        in_specs=[pl.BlockSpec((1,) + vecs.shape[1:], lambda q: (q, 0, 0)),
                  pl.BlockSpec((mats.shape[0], 1, LANES, ns), lambda q: (0, q, 0, 0)),
                  pl.BlockSpec((1, 1, LANES), lambda q: (q, 0, 0)),
                  pl.BlockSpec((1, m, cw), lambda q: (q, 0, 0))],
        out_specs=pl.BlockSpec((1, m, cw), lambda q: (q, 0, 0)),
        out_shape=jax.ShapeDtypeStruct((n_slab, m, cw), BF16),
        scratch_shapes=[pltpu.VMEM((cw, cw), BF16),
                        pltpu.VMEM((cw, 2 * ns), BF16),
                        pltpu.VMEM((cw, 2 * ns), BF16)],
        compiler_params=pltpu.CompilerParams(
            dimension_semantics=("arbitrary",), vmem_limit_bytes=VMEM_LIMIT_BYTES),
        name="ssm",
    )(vecs, mats, d_skip, u_flat)


def _causal_dwconv(gbuf, cw_ref, cb_ref, cbuf, rows):
    base = CONV_HALO - (CONV_KERNEL - 1)
    half = CONV_ROW_BLOCK // 2
    for j in range(gbuf.shape[0]):
        ln = slice(j * LANES, (j + 1) * LANES)
        for i in range(rows // CONV_ROW_BLOCK):
            r0 = i * CONV_ROW_BLOCK
            for parity in range(2):
                acc = jnp.zeros((half, LANES), F32) + cb_ref[:, ln]
                for k in range(CONV_KERNEL):
                    taps = gbuf[j, pl.ds(r0 + parity + base + k, half, stride=2), :]
                    acc = acc + cw_ref[k:k + 1, ln] * taps
                cbuf[j, pl.ds(r0 + parity, half, stride=2), :] = acc


def _back_kernel(x_ref, yzp_ref, szh_ref, ys_ref, wglu32_ref, wo032_ref, n1_ref, wi1_ref, cw_ref, cb_ref,
                 lg_ref, lb_ref, wo132_ref, fn_ref, o_ref, gbuf, cbuf, ybuf, wglu_ref, wo0_ref, wo1_ref):
    t = pl.program_id(1)

    @pl.when((t == 0) & (pl.program_id(0) == 0))
    def _():
        wglu_ref[...] = wglu32_ref[...].astype(BF16)
        wo0_ref[...] = wo032_ref[...].astype(BF16)
        wo1_ref[...] = wo132_ref[...].astype(BF16)

    rows = x_ref.shape[1]
    n_slab = ys_ref.shape[0]
    c_slab = gbuf.shape[0]
    ch = c_slab * LANES
    chunks = rows // CHUNK

    for q in range(n_slab):
        for st in range(CHUNK):
            ybuf[q, pl.ds(st, chunks, stride=CHUNK), :] = (
                ys_ref[q, :, st * LANES:(st + 1) * LANES].astype(F32))

    @pl.when(t == 0)
    def _():
        gbuf[:, 0:CONV_HALO, :] = jnp.zeros((c_slab, CONV_HALO, LANES), F32)

    sub = rows // ROW_SUBTILES
    x1s, szs = [], []
    for hh in range(ROW_SUBTILES):
        rs = slice(hh * sub, (hh + 1) * sub)
        y_ssm = jnp.concatenate([ybuf[q, rs, :] for q in range(n_slab)], axis=1).astype(BF16)
        gv = jnp.dot(y_ssm, wglu_ref[...], preferred_element_type=F32)
        half = gv.shape[1] // 2
        y_s5 = gv[:, :half] * _sigmoid(gv[:, half:])
        yz_ssm = (y_s5 * szh_ref[0, rs, :].astype(F32)).astype(BF16)
        y0 = jnp.concatenate([yzp_ref[0, rs, :], yz_ssm], axis=1)
        x1 = x_ref[0, rs, :] + jnp.dot(y0, wo0_ref[...], preferred_element_type=F32)

        h1 = _rmsnorm(x1, n1_ref[...]).astype(BF16)
        proj = jnp.dot(h1, wi1_ref[...], preferred_element_type=F32)
        glu = proj[:, :ch] * _sigmoid(proj[:, ch:2 * ch])
        for j in range(c_slab):
            gbuf[j, CONV_HALO + hh * sub:CONV_HALO + (hh + 1) * sub, :] = glu[:, j * LANES:(j + 1) * LANES]
        szs.append(_silu(proj[:, 2 * ch:]))
        x1s.append(x1)

    _causal_dwconv(gbuf, cw_ref, cb_ref, cbuf, rows)
    for j in range(c_slab):
        gbuf[j, 0:CONV_HALO, :] = gbuf[j, rows:rows + CONV_HALO, :]

    for hh in range(ROW_SUBTILES):
        rs = slice(hh * sub, (hh + 1) * sub)
        acc = jnp.concatenate([cbuf[j, rs, :] for j in range(c_slab)], axis=1)
        mu = jnp.mean(acc, axis=-1, keepdims=True)
        cc = acc - mu
        var = jnp.mean(cc * cc, axis=-1, keepdims=True)
        cn = cc * lax.rsqrt(var + LN_EPS) * lg_ref[...] + lb_ref[...]
        y1 = (_silu(cn) * szs[hh]).astype(BF16)
        x2 = x1s[hh] + jnp.dot(y1, wo1_ref[...], preferred_element_type=F32)
        o_ref[0, rs, :] = _rmsnorm(x2, fn_ref[...])


def _back_call(x, yzp, szh, ys, wglu, wo0, n1, wi1, cw, cb, lg, lb, wo1, fn, *, rows):
    b, s, d = x.shape
    n_slab = ys.shape[0]
    grid = (b, s // rows)
    const2 = lambda i, j: (0, 0)
    full = lambda a: pl.BlockSpec(a.shape, const2)
    once = lambda a: pl.BlockSpec(a.shape, const2, pipeline_mode=pl.Buffered(1))
    return pl.pallas_call(
        _back_kernel,
        grid=grid,
        in_specs=[
            pl.BlockSpec((1, rows, d), lambda i, j: (i, j, 0)),
            pl.BlockSpec((1, rows, yzp.shape[2]), lambda i, j: (i, j, 0)),
            pl.BlockSpec((1, rows, szh.shape[2]), lambda i, j: (i, j, 0)),
            pl.BlockSpec((n_slab, rows // CHUNK, CHUNK * LANES), lambda i, j, n=s // rows: (0, i * n + j, 0)),
            once(wglu), once(wo0), full(n1), once(wi1), full(cw), full(cb), full(lg), full(lb),
            once(wo1), full(fn),
        ],
        out_specs=pl.BlockSpec((1, rows, d), lambda i, j: (i, j, 0)),
        out_shape=jax.ShapeDtypeStruct((b, s, d), x.dtype),
        scratch_shapes=[pltpu.VMEM((cw.shape[1] // LANES, rows + CONV_HALO, LANES), F32),
                        pltpu.VMEM((cw.shape[1] // LANES, rows, LANES), F32),
                        pltpu.VMEM((n_slab, rows, LANES), F32),
                        pltpu.VMEM(wglu.shape, BF16), pltpu.VMEM(wo0.shape, BF16), pltpu.VMEM(wo1.shape, BF16)],
        compiler_params=pltpu.CompilerParams(
            dimension_semantics=("arbitrary", "arbitrary"), vmem_limit_bytes=VMEM_LIMIT_BYTES),
        name="back",
    )(x, yzp, szh, ys, wglu, wo0, n1, wi1, cw, cb, lg, lb, wo1, fn)


def _ssm_param_layout(log_dt, a_re, a_im, b_re, b_im, c_re, c_im, n_slab):
    g, p = a_re.shape
    h = c_re.shape[1]
    per = g // n_slab
    vecs = jnp.stack([jnp.repeat(log_dt, p), a_re.reshape(-1), a_im.reshape(-1)]).reshape(3, n_slab, per * p)
    vecs = jnp.pad(jnp.swapaxes(vecs, 0, 1), ((0, 0), (0, 5), (0, 0)))
    mats = jnp.stack([jnp.swapaxes(b_re, 1, 2), jnp.swapaxes(b_im, 1, 2), c_re, c_im])
    mats = jnp.tile(mats.reshape(4, n_slab, per * h, p), (1, 1, 1, per))
    return vecs, mats


def _forward(x, even_norm, even_w_in, pool_w, pool_scale, ssm_log_dt, ssm_a_re, ssm_a_im, ssm_b_re,
             ssm_b_im, ssm_c_re, ssm_c_im, ssm_d, ssm_w_glu, even_w_out, odd_norm, odd_w_in, conv_w,
             conv_b, conv_ln_g, conv_ln_b, odd_w_out, final_norm, *, front_rows, back_rows):
    b, s, d = x.shape
    groups = ssm_log_dt.shape[1]
    n_slab = groups // GROUPS_PER_SLAB

    yzp, szh, u = _front_call(
        x, even_norm[0][None, :], even_w_in[0], pool_w[0].astype(BF16),
        pool_scale[0][None, :], rows=front_rows)

    vecs, mats = _ssm_param_layout(ssm_log_dt[0], ssm_a_re[0], ssm_a_im[0], ssm_b_re[0], ssm_b_im[0],
                                   ssm_c_re[0], ssm_c_im[0], n_slab)
    ys = _ssm_call(vecs, mats, ssm_d[0].reshape(n_slab, 1, LANES), u,
                   chunks_per_seq=s // CHUNK, tile=2 * LANES)

    return _back_call(
        x, yzp, szh, ys, ssm_w_glu[0], even_w_out[0], odd_norm[0][None, :],
        odd_w_in[0].astype(BF16), conv_w[0], conv_b[0][None, :], conv_ln_g[0][None, :],
        conv_ln_b[0][None, :], odd_w_out[0], final_norm[None, :], rows=back_rows)


def kernel(x, even_norm, even_w_in, pool_w, pool_scale, ssm_log_dt, ssm_a_re, ssm_a_im, ssm_b_re, ssm_b_im, ssm_c_re, ssm_c_im, ssm_d, ssm_w_glu, even_w_out, odd_norm, odd_w_in, conv_w, conv_b, conv_ln_g, conv_ln_b, odd_w_out, final_norm):
    return _forward(x, even_norm, even_w_in, pool_w, pool_scale, ssm_log_dt, ssm_a_re, ssm_a_im,
                    ssm_b_re, ssm_b_im, ssm_c_re, ssm_c_im, ssm_d, ssm_w_glu, even_w_out, odd_norm,
                    odd_w_in, conv_w, conv_b, conv_ln_g, conv_ln_b, odd_w_out, final_norm,
                    front_rows=2048, back_rows=512)
```

```python
import functools

import jax
import jax.numpy as jnp
from jax import lax
from jax.experimental import pallas as pl
from jax.experimental.pallas import tpu as pltpu

F32 = jnp.float32
BF16 = jnp.bfloat16

RMS_EPS = 1e-6
LN_EPS = 1e-5
POOL_WINDOWS = (2, 4, 8, 16)
POOL_HALO = 16
SSM_GROUP_DIM = 16
SSM_STATE = 64
CONV_KERNEL = 31
CONV_HALO = 32
LANES = 128
CONV_ROW_BLOCK = 256
CHUNK = 16
FOLD_STRIDE = 4
ROW_SUBTILES = 2
GROUPS_PER_SLAB = LANES // SSM_GROUP_DIM
SLAB_STATE = GROUPS_PER_SLAB * SSM_STATE
VMEM_LIMIT_BYTES = 56 * 1024 * 1024


def _sigmoid(v):
    return 1.0 / (1.0 + jnp.exp(-v))


def _silu(v):
    return v * _sigmoid(v)


def _rmsnorm(v, gain):
    ms = jnp.mean(v * v, axis=-1, keepdims=True)
    return v * lax.rsqrt(ms + RMS_EPS) * gain


def _front_kernel(x_ref, g_ref, w32_ref, pw_ref, ps_ref, yzp_ref, szh_ref, u_ref, pbuf, ubuf, w_ref, tbuf):
    t = pl.program_id(1)

    @pl.when((t == 0) & (pl.program_id(0) == 0))
    def _():
        w_ref[...] = w32_ref[...].astype(BF16)

    rows = x_ref.shape[1]
    pool_width = pbuf.shape[1]
    n_slab = u_ref.shape[0]

    @pl.when(t == 0)
    def _():
        pbuf[0:POOL_HALO, :] = jnp.zeros((POOL_HALO, pool_width), F32)

    mix = w_ref.shape[1] // 2
    sub = rows // ROW_SUBTILES
    for hh in range(ROW_SUBTILES):
        r0 = hh * sub
        rs = slice(r0, r0 + sub)
        h = _rmsnorm(x_ref[0, rs, :], g_ref[...]).astype(BF16)
        proj = jnp.dot(h, w_ref[...], preferred_element_type=F32)
        sz = _silu(proj[:, mix:])

        for q in range(n_slab):
            lo = pool_width + q * LANES
            ubuf[q, rs, :] = proj[:, lo:lo + LANES]
            for s4 in range(FOLD_STRIDE):
                tbuf[s4] = ubuf[q, pl.ds(r0 + s4, sub // FOLD_STRIDE, stride=FOLD_STRIDE), :]
            for st in range(CHUNK):
                u_ref[q, r0 // CHUNK:(r0 + sub) // CHUNK, st * LANES:(st + 1) * LANES] = (
                    tbuf[st % FOLD_STRIDE, pl.ds(st // FOLD_STRIDE, sub // CHUNK, stride=FOLD_STRIDE), :]
                    .astype(BF16))
        szh_ref[0, rs, :] = sz[:, pool_width:].astype(BF16)

        pbuf[POOL_HALO + r0:POOL_HALO + r0 + sub, :] = proj[:, :pool_width]
        pos = (t * rows + r0 + lax.broadcasted_iota(jnp.int32, (sub + POOL_HALO, 1), 0)
               - (POOL_HALO - 1))
        for gi, win in enumerate(POOL_WINDOWS):
            sl = slice(gi * LANES, (gi + 1) * LANES)
            ext = pbuf[r0:r0 + sub + POOL_HALO, sl]
            acc = ext
            shift = 1
            while shift < win:
                acc = acc + pltpu.roll(acc, shift, 0)
                shift *= 2
            inv = 1.0 / jnp.clip(pos, 1, win).astype(F32)
            pooled = (acc * inv - ext)[POOL_HALO:, :]
            y = jnp.dot(pooled.astype(BF16), pw_ref[gi], preferred_element_type=F32) * ps_ref[:, sl]
            yzp_ref[0, rs, sl] = (y * sz[:, sl]).astype(BF16)
    pbuf[0:POOL_HALO, :] = pbuf[rows:rows + POOL_HALO, :]


def _front_call(x, gain, w_in, pool_w, pool_scale, *, rows):
    b, s, d = x.shape
    pool_width = pool_scale.shape[1]
    mix = w_in.shape[1] // 2
    n_slab = (mix - pool_width) // LANES
    grid = (b, s // rows)
    const2 = lambda i, j: (0, 0)
    return pl.pallas_call(
        _front_kernel,
        grid=grid,
        in_specs=[
            pl.BlockSpec((1, rows, d), lambda i, j: (i, j, 0)),
            pl.BlockSpec((1, d), const2),
            pl.BlockSpec(w_in.shape, const2, pipeline_mode=pl.Buffered(1)),
            pl.BlockSpec(pool_w.shape, lambda i, j: (0, 0, 0)),
            pl.BlockSpec((1, pool_width), const2),
        ],
        out_specs=[
            pl.BlockSpec((1, rows, pool_width), lambda i, j: (i, j, 0)),
            pl.BlockSpec((1, rows, mix - pool_width), lambda i, j: (i, j, 0)),
            pl.BlockSpec((n_slab, rows // CHUNK, CHUNK * LANES), lambda i, j, n=s // rows: (0, i * n + j, 0)),
        ],
        out_shape=[
            jax.ShapeDtypeStruct((b, s, pool_width), BF16),
            jax.ShapeDtypeStruct((b, s, mix - pool_width), BF16),
            jax.ShapeDtypeStruct((n_slab, b * s // CHUNK, CHUNK * LANES), BF16),
        ],
        scratch_shapes=[pltpu.VMEM((rows + POOL_HALO, pool_width), F32),
                        pltpu.VMEM((n_slab, rows, LANES), F32),
                        pltpu.VMEM(w_in.shape, BF16),
                        pltpu.VMEM((FOLD_STRIDE, rows // ROW_SUBTILES // FOLD_STRIDE, LANES), F32)],
        compiler_params=pltpu.CompilerParams(
            dimension_semantics=("arbitrary", "arbitrary"), vmem_limit_bytes=VMEM_LIMIT_BYTES),
        name="front",
    )(x, gain, w_in, pool_w, pool_scale)


def _ssm_tables(vec_ref, mat_ref, d_ref, toe, wst, wout, tile):
    ns = SLAB_STATE
    dt = jnp.exp(vec_ref[0, 0:1, :])
    a_re, a_im = vec_ref[0, 1:2, :], vec_ref[0, 2:3, :]
    lam, ang = a_re * dt, a_im * dt
    n_pow = CHUNK + 1
    n_rows = -(-n_pow // 8) * 8
    step = lax.broadcasted_iota(jnp.int32, (n_rows, ns), 0).astype(F32)
    mag = jnp.exp(step * lam)
    pw_re = mag * jnp.cos(step * ang)
    pw_im = mag * jnp.sin(step * ang)

    den = a_re * a_re + a_im * a_im
    nr, ni = pw_re[1:2] - 1.0, pw_im[1:2]
    k_re = (nr * a_re + ni * a_im) / den
    k_im = (ni * a_re - nr * a_im) / den

    row_g = lax.broadcasted_iota(jnp.int32, (LANES, ns), 0) // SSM_GROUP_DIM
    lane_g = lax.broadcasted_iota(jnp.int32, (LANES, ns), 1) // SSM_STATE
    same = row_g == lane_g
    b_re, b_im = mat_ref[0, 0], mat_ref[1, 0]
    bb_re = jnp.where(same, k_re * b_re - k_im * b_im, 0.0)
    bb_im = jnp.where(same, k_re * b_im + k_im * b_re, 0.0)
    c_re = jnp.where(same, mat_ref[2, 0], 0.0)
    c_im = jnp.where(same, mat_ref[3, 0], 0.0)
    c_bf = jnp.concatenate([c_re, -c_im], axis=1).astype(BF16)

    eye = (lax.broadcasted_iota(jnp.int32, (LANES, LANES), 0)
           == lax.broadcasted_iota(jnp.int32, (LANES, LANES), 1))
    zero_blk = jnp.zeros((LANES, LANES), BF16)
    steps_per_tile = tile // LANES
    for n in range(CHUNK):
        p_re, p_im = pw_re[n:n + 1], pw_im[n:n + 1]
        e_cat = jnp.concatenate([p_re * bb_re - p_im * bb_im, p_re * bb_im + p_im * bb_re], axis=1)
        s = CHUNK - 1 - n
        e_bf = e_cat.astype(BF16)
        wst[s * LANES:(s + 1) * LANES, :] = e_bf
        k_blk = lax.dot_general(e_bf, c_bf, (((1,), (1,)), ((), ())), preferred_element_type=F32)
        if n == 0:
            k_blk = k_blk + jnp.where(eye, d_ref[0], 0.0)
        k_bf = k_blk.astype(BF16)
        for s0 in range(CHUNK - n):
            t0 = s0 + n
            toe[s0 * LANES:(s0 + 1) * LANES, t0 * LANES:(t0 + 1) * LANES] = k_bf
    for s0 in range(CHUNK):
        for t0 in range(s0 // steps_per_tile * steps_per_tile, s0):
            toe[s0 * LANES:(s0 + 1) * LANES, t0 * LANES:(t0 + 1) * LANES] = zero_blk
    for t0 in range(CHUNK):
        p_re, p_im = pw_re[t0 + 1:t0 + 2], pw_im[t0 + 1:t0 + 2]
        f_cat = jnp.concatenate([c_re * p_re - c_im * p_im, -(c_re * p_im + c_im * p_re)], axis=1)
        wout[t0 * LANES:(t0 + 1) * LANES, :] = f_cat.astype(BF16)
    return pw_re[CHUNK:CHUNK + 1], pw_im[CHUNK:CHUNK + 1]


def _ssm_kernel(vec_ref, mat_ref, d_ref, u_ref, y_ref, toe, wst, wout, *, chunks_per_seq, tile):
    ns = SLAB_STATE
    p_re, p_im = _ssm_tables(vec_ref, mat_ref, d_ref, toe, wst, wout, tile)

    s_all = jnp.dot(u_ref[0], wst[...], preferred_element_type=F32)
    x_re, x_im = s_all[:, :ns], s_all[:, ns:]
    m = s_all.shape[0]
    idx = lax.broadcasted_iota(jnp.int32, (m, 1), 0) % chunks_per_seq
    shift = 1
    while shift < chunks_per_seq:
        live = idx >= shift
        y_re = jnp.where(live, pltpu.roll(x_re, shift, 0), 0.0)
        y_im = jnp.where(live, pltpu.roll(x_im, shift, 0), 0.0)
        x_re, x_im = x_re + p_re * y_re - p_im * y_im, x_im + p_re * y_im + p_im * y_re
        p_re, p_im = p_re * p_re - p_im * p_im, 2.0 * p_re * p_im
        shift *= 2
    first = idx >= 1
    xp = jnp.concatenate([jnp.where(first, pltpu.roll(x_re, 1, 0), 0.0),
                          jnp.where(first, pltpu.roll(x_im, 1, 0), 0.0)], axis=1).astype(BF16)

    cw = toe.shape[1]
    for j in range(cw // tile):
        hi = (j + 1) * tile
        y = jnp.dot(u_ref[0, :, :hi], toe[:hi, j * tile:hi], preferred_element_type=F32)
        y = y + lax.dot_general(xp, wout[j * tile:hi, :], (((1,), (1,)), ((), ())),
                                preferred_element_type=F32)
        y_ref[0, :, j * tile:hi] = y.astype(BF16)


def _ssm_call(vecs, mats, d_skip, u_flat, *, chunks_per_seq, tile):
    n_slab, m, cw = u_flat.shape
    ns = SLAB_STATE
    return pl.pallas_call(
        functools.partial(_ssm_kernel, chunks_per_seq=chunks_per_seq, tile=tile),
        grid=(n_slab,),
        in_specs=[pl.BlockSpec((1,) + vecs.shape[1:], lambda q: (q, 0, 0)),
                  pl.BlockSpec((mats.shape[0], 1, LANES, ns), lambda q: (0, q, 0, 0)),
                  pl.BlockSpec((1, 1, LANES), lambda q: (q, 0, 0)),
                  pl.BlockSpec((1, m, cw), lambda q: (q, 0, 0))],
        out_specs=pl.BlockSpec((1, m, cw), lambda q: (q, 0, 0)),
        out_shape=jax.ShapeDtypeStruct((n_slab, m, cw), BF16),
        scratch_shapes=[pltpu.VMEM((cw, cw), BF16),
                        pltpu.VMEM((cw, 2 * ns), BF16),
                        pltpu.VMEM((cw, 2 * ns), BF16)],
        compiler_params=pltpu.CompilerParams(
            dimension_semantics=("arbitrary",), vmem_limit_bytes=VMEM_LIMIT_BYTES),
        name="ssm",
    )(vecs, mats, d_skip, u_flat)


def _causal_dwconv(gbuf, cw_ref, cb_ref, cbuf, rows):
    base = CONV_HALO - (CONV_KERNEL - 1)
    half = CONV_ROW_BLOCK // 2
    for j in range(gbuf.shape[0]):
        ln = slice(j * LANES, (j + 1) * LANES)
        for i in range(rows // CONV_ROW_BLOCK):
            r0 = i * CONV_ROW_BLOCK
            for parity in range(2):
                acc = jnp.zeros((half, LANES), F32) + cb_ref[:, ln]
                for k in range(CONV_KERNEL):
                    taps = gbuf[j, pl.ds(r0 + parity + base + k, half, stride=2), :]
                    acc = acc + cw_ref[k:k + 1, ln] * taps
                cbuf[j, pl.ds(r0 + parity, half, stride=2), :] = acc


def _back_kernel(x_ref, yzp_ref, szh_ref, ys_ref, wglu32_ref, wo032_ref, n1_ref, wi1_ref, cw_ref, cb_ref,
                 lg_ref, lb_ref, wo132_ref, fn_ref, o_ref, gbuf, cbuf, ybuf, wglu_ref, wo0_ref, wo1_ref, tbuf):
    t = pl.program_id(1)

    @pl.when((t == 0) & (pl.program_id(0) == 0))
    def _():
        wglu_ref[...] = wglu32_ref[...].astype(BF16)
        wo0_ref[...] = wo032_ref[...].astype(BF16)
        wo1_ref[...] = wo132_ref[...].astype(BF16)

    rows = x_ref.shape[1]
    n_slab = ys_ref.shape[0]
    c_slab = gbuf.shape[0]
    ch = c_slab * LANES
    chunks = rows // CHUNK

    for q in range(n_slab):
        for st in range(CHUNK):
            tbuf[st % FOLD_STRIDE, pl.ds(st // FOLD_STRIDE, chunks, stride=FOLD_STRIDE), :] = (
                ys_ref[q, :, st * LANES:(st + 1) * LANES].astype(F32))
        for s4 in range(FOLD_STRIDE):
            ybuf[q, pl.ds(s4, rows // FOLD_STRIDE, stride=FOLD_STRIDE), :] = tbuf[s4]

    @pl.when(t == 0)
    def _():
        gbuf[:, 0:CONV_HALO, :] = jnp.zeros((c_slab, CONV_HALO, LANES), F32)

    sub = rows // ROW_SUBTILES
    x1s, szs = [], []
    for hh in range(ROW_SUBTILES):
        rs = slice(hh * sub, (hh + 1) * sub)
        y_ssm = jnp.concatenate([ybuf[q, rs, :] for q in range(n_slab)], axis=1).astype(BF16)
        gv = jnp.dot(y_ssm, wglu_ref[...], preferred_element_type=F32)
        half = gv.shape[1] // 2
        y_s5 = gv[:, :half] * _sigmoid(gv[:, half:])
        yz_ssm = (y_s5 * szh_ref[0, rs, :].astype(F32)).astype(BF16)
        y0 = jnp.concatenate([yzp_ref[0, rs, :], yz_ssm], axis=1)
        x1 = x_ref[0, rs, :] + jnp.dot(y0, wo0_ref[...], preferred_element_type=F32)

        h1 = _rmsnorm(x1, n1_ref[...]).astype(BF16)
        proj = jnp.dot(h1, wi1_ref[...], preferred_element_type=F32)
        glu = proj[:, :ch] * _sigmoid(proj[:, ch:2 * ch])
        for j in range(c_slab):
            gbuf[j, CONV_HALO + hh * sub:CONV_HALO + (hh + 1) * sub, :] = glu[:, j * LANES:(j + 1) * LANES]
        szs.append(_silu(proj[:, 2 * ch:]))
        x1s.append(x1)

    _causal_dwconv(gbuf, cw_ref, cb_ref, cbuf, rows)
    for j in range(c_slab):
        gbuf[j, 0:CONV_HALO, :] = gbuf[j, rows:rows + CONV_HALO, :]

    for hh in range(ROW_SUBTILES):
        rs = slice(hh * sub, (hh + 1) * sub)
        acc = jnp.concatenate([cbuf[j, rs, :] for j in range(c_slab)], axis=1)
        mu = jnp.mean(acc, axis=-1, keepdims=True)
        cc = acc - mu
        var = jnp.mean(cc * cc, axis=-1, keepdims=True)
        cn = cc * lax.rsqrt(var + LN_EPS) * lg_ref[...] + lb_ref[...]
        y1 = (_silu(cn) * szs[hh]).astype(BF16)
        x2 = x1s[hh] + jnp.dot(y1, wo1_ref[...], preferred_element_type=F32)
        o_ref[0, rs, :] = _rmsnorm(x2, fn_ref[...])


def _back_call(x, yzp, szh, ys, wglu, wo0, n1, wi1, cw, cb, lg, lb, wo1, fn, *, rows):
    b, s, d = x.shape
    n_slab = ys.shape[0]
    grid = (b, s // rows)
    const2 = lambda i, j: (0, 0)
    full = lambda a: pl.BlockSpec(a.shape, const2)
    once = lambda a: pl.BlockSpec(a.shape, const2, pipeline_mode=pl.Buffered(1))
    return pl.pallas_call(
        _back_kernel,
        grid=grid,
        in_specs=[
            pl.BlockSpec((1, rows, d), lambda i, j: (i, j, 0)),
            pl.BlockSpec((1, rows, yzp.shape[2]), lambda i, j: (i, j, 0)),
            pl.BlockSpec((1, rows, szh.shape[2]), lambda i, j: (i, j, 0)),
            pl.BlockSpec((n_slab, rows // CHUNK, CHUNK * LANES), lambda i, j, n=s // rows: (0, i * n + j, 0)),
            once(wglu), once(wo0), full(n1), once(wi1), full(cw), full(cb), full(lg), full(lb),
            once(wo1), full(fn),
        ],
        out_specs=pl.BlockSpec((1, rows, d), lambda i, j: (i, j, 0)),
        out_shape=jax.ShapeDtypeStruct((b, s, d), x.dtype),
        scratch_shapes=[pltpu.VMEM((cw.shape[1] // LANES, rows + CONV_HALO, LANES), F32),
                        pltpu.VMEM((cw.shape[1] // LANES, rows, LANES), F32),
                        pltpu.VMEM((n_slab, rows, LANES), F32),
                        pltpu.VMEM(wglu.shape, BF16), pltpu.VMEM(wo0.shape, BF16), pltpu.VMEM(wo1.shape, BF16),
                        pltpu.VMEM((FOLD_STRIDE, rows // FOLD_STRIDE, LANES), F32)],
        compiler_params=pltpu.CompilerParams(
            dimension_semantics=("arbitrary", "arbitrary"), vmem_limit_bytes=VMEM_LIMIT_BYTES),
        name="back",
    )(x, yzp, szh, ys, wglu, wo0, n1, wi1, cw, cb, lg, lb, wo1, fn)


def _ssm_param_layout(log_dt, a_re, a_im, b_re, b_im, c_re, c_im, n_slab):
    g, p = a_re.shape
    h = c_re.shape[1]
    per = g // n_slab
    vecs = jnp.stack([jnp.repeat(log_dt, p), a_re.reshape(-1), a_im.reshape(-1)]).reshape(3, n_slab, per * p)
    vecs = jnp.pad(jnp.swapaxes(vecs, 0, 1), ((0, 0), (0, 5), (0, 0)))
    mats = jnp.stack([jnp.swapaxes(b_re, 1, 2), jnp.swapaxes(b_im, 1, 2), c_re, c_im])
    mats = jnp.tile(mats.reshape(4, n_slab, per * h, p), (1, 1, 1, per))
    return vecs, mats


def _forward(x, even_norm, even_w_in, pool_w, pool_scale, ssm_log_dt, ssm_a_re, ssm_a_im, ssm_b_re,
             ssm_b_im, ssm_c_re, ssm_c_im, ssm_d, ssm_w_glu, even_w_out, odd_norm, odd_w_in, conv_w,
             conv_b, conv_ln_g, conv_ln_b, odd_w_out, final_norm, *, front_rows, back_rows):
    b, s, d = x.shape
    groups = ssm_log_dt.shape[1]
    n_slab = groups // GROUPS_PER_SLAB

    yzp, szh, u = _front_call(
        x, even_norm[0][None, :], even_w_in[0], pool_w[0].astype(BF16),
        pool_scale[0][None, :], rows=front_rows)

    vecs, mats = _ssm_param_layout(ssm_log_dt[0], ssm_a_re[0], ssm_a_im[0], ssm_b_re[0], ssm_b_im[0],
                                   ssm_c_re[0], ssm_c_im[0], n_slab)
    ys = _ssm_call(vecs, mats, ssm_d[0].reshape(n_slab, 1, LANES), u,
                   chunks_per_seq=s // CHUNK, tile=2 * LANES)

    return _back_call(
        x, yzp, szh, ys, ssm_w_glu[0], even_w_out[0], odd_norm[0][None, :],
        odd_w_in[0].astype(BF16), conv_w[0], conv_b[0][None, :], conv_ln_g[0][None, :],
        conv_ln_b[0][None, :], odd_w_out[0], final_norm[None, :], rows=back_rows)


def kernel(x, even_norm, even_w_in, pool_w, pool_scale, ssm_log_dt, ssm_a_re, ssm_a_im, ssm_b_re, ssm_b_im, ssm_c_re, ssm_c_im, ssm_d, ssm_w_glu, even_w_out, odd_norm, odd_w_in, conv_w, conv_b, conv_ln_g, conv_ln_b, odd_w_out, final_norm):
    return _forward(x, even_norm, even_w_in, pool_w, pool_scale, ssm_log_dt, ssm_a_re, ssm_a_im,
                    ssm_b_re, ssm_b_im, ssm_c_re, ssm_c_im, ssm_d, ssm_w_glu, even_w_out, odd_norm,
                    odd_w_in, conv_w, conv_b, conv_ln_g, conv_ln_b, odd_w_out, final_norm,
                    front_rows=2048, back_rows=512)
```

```python
import functools

import jax
import jax.numpy as jnp
from jax import lax
from jax.experimental import pallas as pl
from jax.experimental.pallas import tpu as pltpu

F32 = jnp.float32
BF16 = jnp.bfloat16

RMS_EPS = 1e-6
LN_EPS = 1e-5
POOL_WINDOWS = (2, 4, 8, 16)
POOL_HALO = 16
SSM_GROUP_DIM = 16
SSM_STATE = 64
CONV_KERNEL = 31
CONV_HALO = 32
LANES = 128
CONV_ROW_BLOCK = 256
CHUNK = 16
FOLD_STRIDE = 4
ROW_SUBTILES = 2
GROUPS_PER_SLAB = LANES // SSM_GROUP_DIM
SLAB_STATE = GROUPS_PER_SLAB * SSM_STATE
VMEM_LIMIT_BYTES = 56 * 1024 * 1024


def _sigmoid(v):
    return 1.0 / (1.0 + jnp.exp(-v))


def _silu(v):
    return v * _sigmoid(v)


def _rmsnorm(v, gain):
    ms = jnp.mean(v * v, axis=-1, keepdims=True)
    return v * lax.rsqrt(ms + RMS_EPS) * gain


def _front_kernel(x_ref, g_ref, w32_ref, pw_ref, ps_ref, yzp_ref, szh_ref, u_ref, pbuf, ubuf, w_ref, tbuf):
    t = pl.program_id(1)

    @pl.when((t == 0) & (pl.program_id(0) == 0))
    def _():
        w_ref[...] = w32_ref[...].astype(BF16)

    rows = x_ref.shape[1]
    pool_width = pbuf.shape[1]
    n_slab = u_ref.shape[0]

    @pl.when(t == 0)
    def _():
        pbuf[0:POOL_HALO, :] = jnp.zeros((POOL_HALO, pool_width), F32)

    mix = w_ref.shape[1] // 2
    sub = rows // ROW_SUBTILES
    for hh in range(ROW_SUBTILES):
        r0 = hh * sub
        rs = slice(r0, r0 + sub)
        h = _rmsnorm(x_ref[0, rs, :], g_ref[...]).astype(BF16)
        proj = jnp.dot(h, w_ref[...], preferred_element_type=F32)
        sz = _silu(proj[:, mix:])

        for q in range(n_slab):
            lo = pool_width + q * LANES
            ubuf[q, rs, :] = proj[:, lo:lo + LANES]
            for s4 in range(FOLD_STRIDE):
                tbuf[s4] = ubuf[q, pl.ds(r0 + s4, sub // FOLD_STRIDE, stride=FOLD_STRIDE), :]
            for st in range(CHUNK):
                u_ref[q, r0 // CHUNK:(r0 + sub) // CHUNK, st * LANES:(st + 1) * LANES] = (
                    tbuf[st % FOLD_STRIDE, pl.ds(st // FOLD_STRIDE, sub // CHUNK, stride=FOLD_STRIDE), :]
                    .astype(BF16))
        szh_ref[0, rs, :] = sz[:, pool_width:].astype(BF16)

        pbuf[POOL_HALO + r0:POOL_HALO + r0 + sub, :] = proj[:, :pool_width]
        head_pos = t * rows + 1 + lax.broadcasted_iota(jnp.int32, (POOL_HALO, 1), 0)
        for gi, win in enumerate(POOL_WINDOWS):
            sl = slice(gi * LANES, (gi + 1) * LANES)
            ext = pbuf[r0:r0 + sub + POOL_HALO, sl]
            acc = ext
            shift = 1
            while shift < win:
                acc = acc + pltpu.roll(acc, shift, 0)
                shift *= 2
            acc, ext = acc[POOL_HALO:, :], ext[POOL_HALO:, :]
            if hh == 0:
                head = acc[:POOL_HALO] * (1.0 / jnp.minimum(head_pos, win).astype(F32)) - ext[:POOL_HALO]
                pooled = jnp.concatenate([head, acc[POOL_HALO:] * (1.0 / win) - ext[POOL_HALO:]], axis=0)
            else:
                pooled = acc * (1.0 / win) - ext
            y = jnp.dot(pooled.astype(BF16), pw_ref[gi], preferred_element_type=F32) * ps_ref[:, sl]
            yzp_ref[0, rs, sl] = (y * sz[:, sl]).astype(BF16)
    pbuf[0:POOL_HALO, :] = pbuf[rows:rows + POOL_HALO, :]


def _front_call(x, gain, w_in, pool_w, pool_scale, *, rows):
    b, s, d = x.shape
    pool_width = pool_scale.shape[1]
    mix = w_in.shape[1] // 2
    n_slab = (mix - pool_width) // LANES
    grid = (b, s // rows)
    const2 = lambda i, j: (0, 0)
    return pl.pallas_call(
        _front_kernel,
        grid=grid,
        in_specs=[
            pl.BlockSpec((1, rows, d), lambda i, j: (i, j, 0)),
            pl.BlockSpec((1, d), const2),
            pl.BlockSpec(w_in.shape, const2, pipeline_mode=pl.Buffered(1)),
            pl.BlockSpec(pool_w.shape, lambda i, j: (0, 0, 0)),
            pl.BlockSpec((1, pool_width), const2),
        ],
        out_specs=[
            pl.BlockSpec((1, rows, pool_width), lambda i, j: (i, j, 0)),
            pl.BlockSpec((1, rows, mix - pool_width), lambda i, j: (i, j, 0)),
            pl.BlockSpec((n_slab, rows // CHUNK, CHUNK * LANES), lambda i, j, n=s // rows: (0, i * n + j, 0)),
        ],
        out_shape=[
            jax.ShapeDtypeStruct((b, s, pool_width), BF16),
            jax.ShapeDtypeStruct((b, s, mix - pool_width), BF16),
            jax.ShapeDtypeStruct((n_slab, b * s // CHUNK, CHUNK * LANES), BF16),
        ],
        scratch_shapes=[pltpu.VMEM((rows + POOL_HALO, pool_width), F32),
                        pltpu.VMEM((n_slab, rows, LANES), F32),
                        pltpu.VMEM(w_in.shape, BF16),
                        pltpu.VMEM((FOLD_STRIDE, rows // ROW_SUBTILES // FOLD_STRIDE, LANES), F32)],
        compiler_params=pltpu.CompilerParams(
            dimension_semantics=("arbitrary", "arbitrary"), vmem_limit_bytes=VMEM_LIMIT_BYTES),
        name="front",
    )(x, gain, w_in, pool_w, pool_scale)


def _ssm_tables(vec_ref, mat_ref, d_ref, toe, wst, wout, tile):
    ns = SLAB_STATE
    dt = jnp.exp(vec_ref[0, 0:1, :])
    a_re, a_im = vec_ref[0, 1:2, :], vec_ref[0, 2:3, :]
    lam, ang = a_re * dt, a_im * dt
    n_pow = CHUNK + 1
    n_rows = -(-n_pow // 8) * 8
    step = lax.broadcasted_iota(jnp.int32, (n_rows, ns), 0).astype(F32)
    mag = jnp.exp(step * lam)
    pw_re = mag * jnp.cos(step * ang)
    pw_im = mag * jnp.sin(step * ang)

    den = a_re * a_re + a_im * a_im
    nr, ni = pw_re[1:2] - 1.0, pw_im[1:2]
    k_re = (nr * a_re + ni * a_im) / den
    k_im = (ni * a_re - nr * a_im) / den

    row_g = lax.broadcasted_iota(jnp.int32, (LANES, ns), 0) // SSM_GROUP_DIM
    lane_g = lax.broadcasted_iota(jnp.int32, (LANES, ns), 1) // SSM_STATE
    same = row_g == lane_g
    per = ns // mat_ref.shape[3]
    b_re, b_im = jnp.tile(mat_ref[0, 0], (1, per)), jnp.tile(mat_ref[1, 0], (1, per))
    bb_re = jnp.where(same, k_re * b_re - k_im * b_im, 0.0)
    bb_im = jnp.where(same, k_re * b_im + k_im * b_re, 0.0)
    c_re = jnp.where(same, jnp.tile(mat_ref[2, 0], (1, per)), 0.0)
    c_im = jnp.where(same, jnp.tile(mat_ref[3, 0], (1, per)), 0.0)
    c_bf = jnp.concatenate([c_re, -c_im], axis=1).astype(BF16)

    eye = (lax.broadcasted_iota(jnp.int32, (LANES, LANES), 0)
           == lax.broadcasted_iota(jnp.int32, (LANES, LANES), 1))
    zero_blk = jnp.zeros((LANES, LANES), BF16)
    steps_per_tile = tile // LANES
    for n in range(CHUNK):
        p_re, p_im = pw_re[n:n + 1], pw_im[n:n + 1]
        e_cat = jnp.concatenate([p_re * bb_re - p_im * bb_im, p_re * bb_im + p_im * bb_re], axis=1)
        s = CHUNK - 1 - n
        e_bf = e_cat.astype(BF16)
        wst[s * LANES:(s + 1) * LANES, :] = e_bf
        k_blk = lax.dot_general(e_bf, c_bf, (((1,), (1,)), ((), ())), preferred_element_type=F32)
        if n == 0:
            k_blk = k_blk + jnp.where(eye, d_ref[0], 0.0)
        k_bf = k_blk.astype(BF16)
        for s0 in range(CHUNK - n):
            t0 = s0 + n
            toe[s0 * LANES:(s0 + 1) * LANES, t0 * LANES:(t0 + 1) * LANES] = k_bf
    for s0 in range(CHUNK):
        for t0 in range(s0 // steps_per_tile * steps_per_tile, s0):
            toe[s0 * LANES:(s0 + 1) * LANES, t0 * LANES:(t0 + 1) * LANES] = zero_blk
    for t0 in range(CHUNK):
        p_re, p_im = pw_re[t0 + 1:t0 + 2], pw_im[t0 + 1:t0 + 2]
        f_cat = jnp.concatenate([c_re * p_re - c_im * p_im, -(c_re * p_im + c_im * p_re)], axis=1)
        wout[t0 * LANES:(t0 + 1) * LANES, :] = f_cat.astype(BF16)
    return pw_re[CHUNK:CHUNK + 1], pw_im[CHUNK:CHUNK + 1]


def _ssm_kernel(vec_ref, mat_ref, d_ref, u_ref, y_ref, toe, wst, wout, *, chunks_per_seq, tile):
    ns = SLAB_STATE
    p_re, p_im = _ssm_tables(vec_ref, mat_ref, d_ref, toe, wst, wout, tile)

    s_all = jnp.dot(u_ref[0], wst[...], preferred_element_type=F32)
    x_re, x_im = s_all[:, :ns], s_all[:, ns:]
    m = s_all.shape[0]
    idx = lax.broadcasted_iota(jnp.int32, (m, 1), 0) % chunks_per_seq
    shift = 1
    while shift < chunks_per_seq:
        live = idx >= shift
        y_re = jnp.where(live, pltpu.roll(x_re, shift, 0), 0.0)
        y_im = jnp.where(live, pltpu.roll(x_im, shift, 0), 0.0)
        x_re, x_im = x_re + p_re * y_re - p_im * y_im, x_im + p_re * y_im + p_im * y_re
        p_re, p_im = p_re * p_re - p_im * p_im, 2.0 * p_re * p_im
        shift *= 2
    first = idx >= 1
    xp = jnp.concatenate([jnp.where(first, pltpu.roll(x_re, 1, 0), 0.0),
                          jnp.where(first, pltpu.roll(x_im, 1, 0), 0.0)], axis=1).astype(BF16)

    cw = toe.shape[1]
    for j in range(cw // tile):
        hi = (j + 1) * tile
        y = jnp.dot(u_ref[0, :, :hi], toe[:hi, j * tile:hi], preferred_element_type=F32)
        y = y + lax.dot_general(xp, wout[j * tile:hi, :], (((1,), (1,)), ((), ())),
                                preferred_element_type=F32)
        y_ref[0, :, j * tile:hi] = y.astype(BF16)


def _ssm_call(vecs, mats, d_skip, u_flat, *, chunks_per_seq, tile):
    n_slab, m, cw = u_flat.shape
    ns = SLAB_STATE
    return pl.pallas_call(
        functools.partial(_ssm_kernel, chunks_per_seq=chunks_per_seq, tile=tile),
        grid=(n_slab,),
        in_specs=[pl.BlockSpec((1,) + vecs.shape[1:], lambda q: (q, 0, 0)),
                  pl.BlockSpec((mats.shape[0], 1, LANES, mats.shape[3]), lambda q: (0, q, 0, 0)),
                  pl.BlockSpec((1, 1, LANES), lambda q: (q, 0, 0)),
                  pl.BlockSpec((1, m, cw), lambda q: (q, 0, 0))],
        out_specs=pl.BlockSpec((1, m, cw), lambda q: (q, 0, 0)),
        out_shape=jax.ShapeDtypeStruct((n_slab, m, cw), BF16),
        scratch_shapes=[pltpu.VMEM((cw, cw), BF16),
                        pltpu.VMEM((cw, 2 * ns), BF16),
                        pltpu.VMEM((cw, 2 * ns), BF16)],
        compiler_params=pltpu.CompilerParams(
            dimension_semantics=("arbitrary",), vmem_limit_bytes=VMEM_LIMIT_BYTES),
        name="ssm",
    )(vecs, mats, d_skip, u_flat)


def _causal_dwconv(gbuf, cw_ref, cb_ref, cbuf, rows):
    base = CONV_HALO - (CONV_KERNEL - 1)
    half = CONV_ROW_BLOCK // 2
    for j in range(gbuf.shape[0]):
        ln = slice(j * LANES, (j + 1) * LANES)
        for i in range(rows // CONV_ROW_BLOCK):
            r0 = i * CONV_ROW_BLOCK
            for parity in range(2):
                acc = jnp.zeros((half, LANES), F32) + cb_ref[:, ln]
                for k in range(CONV_KERNEL):
                    taps = gbuf[j, pl.ds(r0 + parity + base + k, half, stride=2), :]
                    acc = acc + cw_ref[k:k + 1, ln] * taps
                cbuf[j, pl.ds(r0 + parity, half, stride=2), :] = acc


def _back_kernel(x_ref, yzp_ref, szh_ref, ys_ref, wglu32_ref, wo032_ref, n1_ref, wi1_ref, cw_ref, cb_ref,
                 lg_ref, lb_ref, wo132_ref, fn_ref, o_ref, gbuf, cbuf, ybuf, wglu_ref, wo0_ref, wo1_ref, tbuf):
    t = pl.program_id(1)

    @pl.when((t == 0) & (pl.program_id(0) == 0))
    def _():
        wglu_ref[...] = wglu32_ref[...].astype(BF16)
        wo0_ref[...] = wo032_ref[...].astype(BF16)
        wo1_ref[...] = wo132_ref[...].astype(BF16)

    rows = x_ref.shape[1]
    n_slab = ys_ref.shape[0]
    c_slab = gbuf.shape[0]
    ch = c_slab * LANES
    chunks = rows // CHUNK

    for q in range(n_slab):
        for st in range(CHUNK):
            tbuf[st % FOLD_STRIDE, pl.ds(st // FOLD_STRIDE, chunks, stride=FOLD_STRIDE), :] = (
                ys_ref[q, :, st * LANES:(st + 1) * LANES].astype(F32))
        for s4 in range(FOLD_STRIDE):
            ybuf[q, pl.ds(s4, rows // FOLD_STRIDE, stride=FOLD_STRIDE), :] = tbuf[s4]

    @pl.when(t == 0)
    def _():
        gbuf[:, 0:CONV_HALO, :] = jnp.zeros((c_slab, CONV_HALO, LANES), F32)

    sub = rows // ROW_SUBTILES
    x1s, szs = [], []
    for hh in range(ROW_SUBTILES):
        rs = slice(hh * sub, (hh + 1) * sub)
        y_ssm = jnp.concatenate([ybuf[q, rs, :] for q in range(n_slab)], axis=1).astype(BF16)
        gv = jnp.dot(y_ssm, wglu_ref[...], preferred_element_type=F32)
        half = gv.shape[1] // 2
        y_s5 = gv[:, :half] * _sigmoid(gv[:, half:])
        yz_ssm = (y_s5 * szh_ref[0, rs, :].astype(F32)).astype(BF16)
        y0 = jnp.concatenate([yzp_ref[0, rs, :], yz_ssm], axis=1)
        x1 = x_ref[0, rs, :] + jnp.dot(y0, wo0_ref[...], preferred_element_type=F32)

        h1 = _rmsnorm(x1, n1_ref[...]).astype(BF16)
        proj = jnp.dot(h1, wi1_ref[...], preferred_element_type=F32)
        glu = proj[:, :ch] * _sigmoid(proj[:, ch:2 * ch])
        for j in range(c_slab):
            gbuf[j, CONV_HALO + hh * sub:CONV_HALO + (hh + 1) * sub, :] = glu[:, j * LANES:(j + 1) * LANES]
        szs.append(_silu(proj[:, 2 * ch:]))
        x1s.append(x1)

    _causal_dwconv(gbuf, cw_ref, cb_ref, cbuf, rows)
    for j in range(c_slab):
        gbuf[j, 0:CONV_HALO, :] = gbuf[j, rows:rows + CONV_HALO, :]

    for hh in range(ROW_SUBTILES):
        rs = slice(hh * sub, (hh + 1) * sub)
        acc = jnp.concatenate([cbuf[j, rs, :] for j in range(c_slab)], axis=1)
        mu = jnp.mean(acc, axis=-1, keepdims=True)
        cc = acc - mu
        var = jnp.mean(cc * cc, axis=-1, keepdims=True)
        cn = cc * lax.rsqrt(var + LN_EPS) * lg_ref[...] + lb_ref[...]
        y1 = (_silu(cn) * szs[hh]).astype(BF16)
        x2 = x1s[hh] + jnp.dot(y1, wo1_ref[...], preferred_element_type=F32)
        o_ref[0, rs, :] = _rmsnorm(x2, fn_ref[...])


def _back_call(x, yzp, szh, ys, wglu, wo0, n1, wi1, cw, cb, lg, lb, wo1, fn, *, rows):
    b, s, d = x.shape
    n_slab = ys.shape[0]
    grid = (b, s // rows)
    const2 = lambda i, j: (0, 0)
    full = lambda a: pl.BlockSpec(a.shape, const2)
    once = lambda a: pl.BlockSpec(a.shape, const2, pipeline_mode=pl.Buffered(1))
    return pl.pallas_call(
        _back_kernel,
        grid=grid,
        in_specs=[
            pl.BlockSpec((1, rows, d), lambda i, j: (i, j, 0)),
            pl.BlockSpec((1, rows, yzp.shape[2]), lambda i, j: (i, j, 0)),
            pl.BlockSpec((1, rows, szh.shape[2]), lambda i, j: (i, j, 0)),
            pl.BlockSpec((n_slab, rows // CHUNK, CHUNK * LANES), lambda i, j, n=s // rows: (0, i * n + j, 0)),
            once(wglu), once(wo0), full(n1), once(wi1), full(cw), full(cb), full(lg), full(lb),
            once(wo1), full(fn),
        ],
        out_specs=pl.BlockSpec((1, rows, d), lambda i, j: (i, j, 0)),
        out_shape=jax.ShapeDtypeStruct((b, s, d), x.dtype),
        scratch_shapes=[pltpu.VMEM((cw.shape[1] // LANES, rows + CONV_HALO, LANES), F32),
                        pltpu.VMEM((cw.shape[1] // LANES, rows, LANES), F32),
                        pltpu.VMEM((n_slab, rows, LANES), F32),
                        pltpu.VMEM(wglu.shape, BF16), pltpu.VMEM(wo0.shape, BF16), pltpu.VMEM(wo1.shape, BF16),
                        pltpu.VMEM((FOLD_STRIDE, rows // FOLD_STRIDE, LANES), F32)],
        compiler_params=pltpu.CompilerParams(
            dimension_semantics=("arbitrary", "arbitrary"), vmem_limit_bytes=VMEM_LIMIT_BYTES),
        name="back",
    )(x, yzp, szh, ys, wglu, wo0, n1, wi1, cw, cb, lg, lb, wo1, fn)


def _ssm_param_layout(log_dt, a_re, a_im, b_re, b_im, c_re, c_im, n_slab):
    g, p = a_re.shape
    h = c_re.shape[1]
    per = g // n_slab
    vecs = jnp.stack([jnp.repeat(log_dt, p), a_re.reshape(-1), a_im.reshape(-1)]).reshape(3, n_slab, per * p)
    vecs = jnp.pad(jnp.swapaxes(vecs, 0, 1), ((0, 0), (0, 5), (0, 0)))
    mats = jnp.stack([jnp.swapaxes(b_re, 1, 2), jnp.swapaxes(b_im, 1, 2), c_re, c_im])
    mats = mats.reshape(4, n_slab, per * h, p)
    return vecs, mats


def _forward(x, even_norm, even_w_in, pool_w, pool_scale, ssm_log_dt, ssm_a_re, ssm_a_im, ssm_b_re,
             ssm_b_im, ssm_c_re, ssm_c_im, ssm_d, ssm_w_glu, even_w_out, odd_norm, odd_w_in, conv_w,
             conv_b, conv_ln_g, conv_ln_b, odd_w_out, final_norm, *, front_rows, back_rows):
    b, s, d = x.shape
    groups = ssm_log_dt.shape[1]
    n_slab = groups // GROUPS_PER_SLAB

    yzp, szh, u = _front_call(
        x, even_norm[0][None, :], even_w_in[0], pool_w[0].astype(BF16),
        pool_scale[0][None, :], rows=front_rows)

    vecs, mats = _ssm_param_layout(ssm_log_dt[0], ssm_a_re[0], ssm_a_im[0], ssm_b_re[0], ssm_b_im[0],
                                   ssm_c_re[0], ssm_c_im[0], n_slab)
    ys = _ssm_call(vecs, mats, ssm_d[0].reshape(n_slab, 1, LANES), u,
                   chunks_per_seq=s // CHUNK, tile=2 * LANES)

    return _back_call(
        x, yzp, szh, ys, ssm_w_glu[0], even_w_out[0], odd_norm[0][None, :],
        odd_w_in[0].astype(BF16), conv_w[0], conv_b[0][None, :], conv_ln_g[0][None, :],
        conv_ln_b[0][None, :], odd_w_out[0], final_norm[None, :], rows=back_rows)


def kernel(x, even_norm, even_w_in, pool_w, pool_scale, ssm_log_dt, ssm_a_re, ssm_a_im, ssm_b_re, ssm_b_im, ssm_c_re, ssm_c_im, ssm_d, ssm_w_glu, even_w_out, odd_norm, odd_w_in, conv_w, conv_b, conv_ln_g, conv_ln_b, odd_w_out, final_norm):
    return _forward(x, even_norm, even_w_in, pool_w, pool_scale, ssm_log_dt, ssm_a_re, ssm_a_im,
                    ssm_b_re, ssm_b_im, ssm_c_re, ssm_c_im, ssm_d, ssm_w_glu, even_w_out, odd_norm,
                    odd_w_in, conv_w, conv_b, conv_ln_g, conv_ln_b, odd_w_out, final_norm,
                    front_rows=2048, back_rows=512)
```
